```python
import math
import jax, jax.numpy as jnp
from jax import lax
import numpy as np

D_MODEL = 1024
BATCH = 8
SEQ = 2048
DEPTH = 4

N_META = 16
N_HEADS = 16
HEAD_DIM = D_MODEL // N_HEADS
Q_BLOCK = 128
CONV_W = 3
D_FF = 4 * D_MODEL
N_MIXERS = 2
N_CONV_LAYERS = (DEPTH + 1) // 2
N_ATTN_LAYERS = DEPTH // 2
RMS_EPS = 1e-6

kernel_name = "hybrid_shortconv_stickbreaking_sqrelu"


def rmsnorm(x, g):
    xf = x.astype(jnp.float32)
    y = xf * lax.rsqrt(jnp.mean(xf * xf, axis=-1, keepdims=True) + RMS_EPS)
    return (y * g.astype(jnp.float32)).astype(x.dtype)


def causal_depthwise_conv(u, conv_w):
    L = u.shape[1]
    up = jnp.pad(u, ((0, 0), (CONV_W - 1, 0), (0, 0)))
    w = conv_w.astype(u.dtype)
    out = up[:, 0:L] * w[0]
    for k in range(1, CONV_W):
        out = out + up[:, k:k + L] * w[k]
    return out


def short_conv_mixer(h, w_in, conv_w, w_out):
    proj = jnp.einsum('bld,de->ble', h, w_in)
    gate_b = proj[..., :D_MODEL]
    gate_c = proj[..., D_MODEL:2 * D_MODEL]
    val = proj[..., 2 * D_MODEL:]
    conv = causal_depthwise_conv(gate_c * val, conv_w)
    return jnp.einsum('bld,de->ble', gate_b * conv, w_out)


def _block_bounds(L):
    bounds = [(0, N_META)]
    n_real = L - N_META
    n_blk = -(-n_real // Q_BLOCK)
    for i in range(n_blk):
        q0 = N_META + i * Q_BLOCK
        bounds.append((q0, min(q0 + Q_BLOCK, L)))
    return bounds


def stick_breaking_attention(q, k, v):
    L = q.shape[2]
    scale = 1.0 / math.sqrt(HEAD_DIM)
    outs = []
    for (q0, q1) in _block_bounds(L):
        qb = q[:, :, q0:q1]
        kb = k[:, :, :q1]
        vb = v[:, :, :q1]
        z = jnp.einsum('bhqd,bhkd->bhqk', qb, kb).astype(jnp.float32) * scale
        t_pos = q0 + jnp.arange(q1 - q0)[:, None]
        s_pos = jnp.arange(q1)[None, :]
        causal = s_pos < t_pos
        log_beta = jax.nn.log_sigmoid(z)
        log_1m = jnp.where(causal, jax.nn.log_sigmoid(-z), 0.0)
        tail = jnp.sum(log_1m, axis=-1, keepdims=True) - jnp.cumsum(log_1m, axis=-1)
        w = jnp.where(causal, jnp.exp(log_beta + tail), 0.0)
        outs.append(jnp.einsum('bhqk,bhkd->bhqd', w.astype(vb.dtype), vb))
    return jnp.concatenate(outs, axis=2)


def stick_breaking_mixer(h, w_qkv, w_out):
    B, L, _ = h.shape
    qkv = jnp.einsum('bld,de->ble', h, w_qkv)
    qkv = qkv.reshape(B, L, 3, N_HEADS, HEAD_DIM)
    q = jnp.transpose(qkv[:, :, 0], (0, 2, 1, 3))
    k = jnp.transpose(qkv[:, :, 1], (0, 2, 1, 3))
    v = jnp.transpose(qkv[:, :, 2], (0, 2, 1, 3))
    o = stick_breaking_attention(q, k, v)
    o = jnp.transpose(o, (0, 2, 1, 3)).reshape(B, L, D_MODEL)
    return jnp.einsum('bld,de->ble', o, w_out)


def sqrelu_mlp(h, w1, w2):
    a = jnp.einsum('bld,df->blf', h, w1)
    a = jnp.square(jax.nn.relu(a))
    return jnp.einsum('blf,fd->bld', a, w2)


def setup_inputs(seed: int = 0) -> dict:
    key = jax.random.key(seed)
    ks = jax.random.split(key, 16)
    f32 = jnp.float32
    D = D_MODEL

    def nrm(k, shape, scale):
        return jax.random.normal(k, shape, f32) * scale

    return {
        "x": nrm(ks[0], (BATCH, SEQ, D), 1.0),
        "meta_tokens": nrm(ks[1], (N_META, D), 1.0),
        "conv_norm": 1.0 + nrm(ks[2], (N_CONV_LAYERS, D), 0.01),
        "conv_w_in": nrm(ks[3], (N_CONV_LAYERS, D, 3 * D), D ** -0.5),
        "conv_w": nrm(ks[4], (N_CONV_LAYERS, CONV_W, D), CONV_W ** -0.5),
        "conv_w_out": nrm(ks[5], (N_CONV_LAYERS, D, D), D ** -0.5),
        "attn_norm": 1.0 + nrm(ks[6], (N_ATTN_LAYERS, D), 0.01),
        "attn_w_qkv": nrm(ks[7], (N_ATTN_LAYERS, D, 3 * D), D ** -0.5),
        "attn_w_out": nrm(ks[8], (N_ATTN_LAYERS, D, D), D ** -0.5),
        "mlp_norm": 1.0 + nrm(ks[9], (DEPTH, D), 0.01),
        "mlp_w1": nrm(ks[10], (DEPTH, D, D_FF), D ** -0.5),
        "mlp_w2": nrm(ks[11], (DEPTH, D_FF, D), D_FF ** -0.5),
        "final_norm": 1.0 + nrm(ks[12], (D,), 0.01),
    }


def reference(x, meta_tokens, conv_norm, conv_w_in, conv_w, conv_w_out,
              attn_norm, attn_w_qkv, attn_w_out, mlp_norm, mlp_w1, mlp_w2,
              final_norm):
    B = x.shape[0]
    meta = jnp.broadcast_to(meta_tokens.astype(x.dtype)[None], (B, N_META, D_MODEL))
    h = jnp.concatenate([meta, x], axis=1)
    for i in range(DEPTH):
        j = i // N_MIXERS
        if i % N_MIXERS == 0:
            h = h + short_conv_mixer(rmsnorm(h, conv_norm[j]), conv_w_in[j],
                                     conv_w[j], conv_w_out[j])
        else:
            h = h + stick_breaking_mixer(rmsnorm(h, attn_norm[j]), attn_w_qkv[j],
                                         attn_w_out[j])
        h = h + sqrelu_mlp(rmsnorm(h, mlp_norm[i]), mlp_w1[i], mlp_w2[i])
    h = rmsnorm(h, final_norm)
    return h[:, N_META:]
```

```python
import functools
import math

import jax
import jax.numpy as jnp
from jax import lax
from jax.experimental import pallas as pl
from jax.experimental.pallas import tpu as pltpu

D_MODEL = 1024
N_META = 16
N_HEADS = 16
HEAD_DIM = D_MODEL // N_HEADS
CONV_W = 3
D_FF = 4 * D_MODEL
RMS_EPS = 1e-6

LANES = 128
SUBLANES = 8
HEADS_PER_GROUP = LANES // HEAD_DIM
N_GROUPS = N_HEADS // HEADS_PER_GROUP
META_ROWS = LANES
ROW_TILE = 512
ATTN_TILE = 256
FF_CHUNK = 1024
VMEM_LIMIT_BYTES = 56 * 1024 * 1024

BF16 = jnp.bfloat16
F32 = jnp.float32


def _dot(a, b):
    return jnp.dot(a, b, preferred_element_type=F32)


def _rmsnorm(x, g):
    return x * lax.rsqrt(jnp.mean(x * x, axis=-1, keepdims=True) + RMS_EPS) * g


def _resident(shape):
    zeros = (0,) * len(shape)
    return pl.BlockSpec(shape, lambda *_: zeros, pipeline_mode=pl.Buffered(1))


def _params(*semantics):
    return pltpu.CompilerParams(dimension_semantics=semantics,
                                vmem_limit_bytes=VMEM_LIMIT_BYTES)


def _conv_mixer_kernel(x_ref, g_ref, w_in_ref, cw_ref, w_out_ref, hist_ref,
                       *rest, tm, tiles_per_seq, hist_row):
    if hist_row is None:
        o_ref, ubuf = rest
    else:
        o_ref, hist_out_ref, ubuf = rest
    i = pl.program_id(0)
    first = (i % tiles_per_seq) == 0

    @pl.when(first)
    def _():
        ubuf[0:SUBLANES, :] = hist_ref[...]

    @pl.when(jnp.logical_not(first))
    def _():
        ubuf[0:SUBLANES, :] = ubuf[tm:tm + SUBLANES, :]

    x = x_ref[...]
    h = _rmsnorm(x, g_ref[...]).astype(BF16)
    d = D_MODEL
    gate_c = _dot(h, w_in_ref[:, d:2 * d])
    val = _dot(h, w_in_ref[:, 2 * d:3 * d])
    ubuf[SUBLANES:SUBLANES + tm, :] = gate_c * val
    if hist_row is not None:
        hist_out_ref[...] = ubuf[hist_row:hist_row + SUBLANES, :]
    cw = cw_ref[...]
    conv = ubuf[SUBLANES:SUBLANES + tm, :] * cw[CONV_W - 1:CONV_W, :]
    for k in range(CONV_W - 1):
        back = CONV_W - 1 - k
        conv = conv + ubuf[SUBLANES - back:SUBLANES - back + tm, :] * cw[k:k + 1, :]
    gate_b = _dot(h, w_in_ref[:, 0:d])
    y = (gate_b * conv).astype(BF16)
    o_ref[...] = x + _dot(y, w_out_ref[...])


def _conv_mixer(x, g, w_in, cw, w_out, hist, *, tm, tiles_per_seq, hist_row=None):
    n, d = x.shape
    kernel = functools.partial(_conv_mixer_kernel, tm=tm,
                               tiles_per_seq=tiles_per_seq, hist_row=hist_row)
    row = pl.BlockSpec((tm, d), lambda i: (i, 0))
    out_shape = jax.ShapeDtypeStruct((n, d), F32)
    out_specs = row
    if hist_row is not None:
        out_shape = (out_shape, jax.ShapeDtypeStruct((SUBLANES, d), F32))
        out_specs = (row, pl.BlockSpec((SUBLANES, d), lambda i: (0, 0)))
    return pl.pallas_call(
        kernel,
        grid=(n // tm,),
        in_specs=[row, _resident((1, d)), _resident((d, 3 * d)),
                  _resident((CONV_W, d)), _resident((d, d)),
                  _resident((SUBLANES, d))],
        out_specs=out_specs,
        out_shape=out_shape,
        scratch_shapes=[pltpu.VMEM((tm + 2 * SUBLANES, d), F32)],
        compiler_params=_params("arbitrary"),
        name="conv_mixer",
    )(x, g, w_in, cw, w_out, hist)


def _mlp_kernel(*refs, with_proj, with_final_norm):
    refs = list(refs)
    x_ref = refs.pop(0)
    if with_proj:
        a_ref, w_o_ref = refs.pop(0), refs.pop(0)
    g_ref, w1_ref, w2_ref = refs.pop(0), refs.pop(0), refs.pop(0)
    if with_final_norm:
        gf_ref = refs.pop(0)
    (o_ref,) = refs

    x = x_ref[...]
    if with_proj:
        x = x + _dot(a_ref[...], w_o_ref[...])
    h = _rmsnorm(x, g_ref[...]).astype(BF16)
    acc = x
    for c in range(D_FF // FF_CHUNK):
        cols = slice(c * FF_CHUNK, (c + 1) * FF_CHUNK)
        a = jnp.maximum(_dot(h, w1_ref[:, cols]), 0.0)
        acc = acc + _dot((a * a).astype(BF16), w2_ref[cols, :])
    if with_final_norm:
        acc = _rmsnorm(acc, gf_ref[...])
    o_ref[...] = acc


def _mlp(x, g, w1, w2, *, tm, attn=None, w_o=None, final_g=None):
    n, d = x.shape
    with_proj = attn is not None
    with_final_norm = final_g is not None
    row = pl.BlockSpec((tm, d), lambda i: (i, 0))
    args, specs = [x], [row]
    if with_proj:
        args += [attn, w_o]
        specs += [row, _resident((d, d))]
    args += [g, w1, w2]
    specs += [_resident((1, d)), _resident((d, D_FF)), _resident((D_FF, d))]
    if with_final_norm:
        args.append(final_g)
        specs.append(_resident((1, d)))
    kernel = functools.partial(_mlp_kernel, with_proj=with_proj,
                               with_final_norm=with_final_norm)
    return pl.pallas_call(
        kernel,
        grid=(n // tm,),
        in_specs=specs,
        out_specs=row,
        out_shape=jax.ShapeDtypeStruct((n, d), F32),
        compiler_params=_params("arbitrary"),
        name="mlp",
    )(*args)


def _qkv_kernel(x_ref, g_ref, w_qv_ref, w_kt_ref, qv_ref, kt_ref):
    d = D_MODEL
    h = _rmsnorm(x_ref[...], g_ref[...]).astype(BF16)
    scale = 1.0 / math.sqrt(HEAD_DIM)
    qv_ref[:, 0:d] = (_dot(h, w_qv_ref[:, 0:d]) * scale).astype(BF16)
    qv_ref[:, d:2 * d] = _dot(h, w_qv_ref[:, d:2 * d]).astype(BF16)
    kt = lax.dot_general(w_kt_ref[...], h, (((1,), (1,)), ((), ())),
                         preferred_element_type=F32)
    kt_ref[...] = kt.astype(BF16)


def _qkv(x, g, w_qv, w_kt, *, tm, seq):
    n, d = x.shape
    tiles_per_seq = seq // tm
    return pl.pallas_call(
        _qkv_kernel,
        grid=(n // tm,),
        in_specs=[pl.BlockSpec((tm, d), lambda i: (i, 0)), _resident((1, d)),
                  _resident((d, 2 * d)), _resident((d, d))],
        out_specs=(pl.BlockSpec((tm, 2 * d), lambda i: (i, 0)),
                   pl.BlockSpec((None, d, tm),
                                lambda i: (i // tiles_per_seq, 0, i % tiles_per_seq))),
        out_shape=(jax.ShapeDtypeStruct((n, 2 * d), BF16),
                   jax.ShapeDtypeStruct((n // seq, d, seq), BF16)),
        compiler_params=_params("arbitrary"),
        name="qkv_proj",
    )(x, g, w_qv, w_kt)


def _attn_kernel(*refs, t, has_meta):
    if has_meta:
        q_ref, kt_ref, v_ref, ktm_ref, vm_ref, u_ref, o_ref, acc_ref, carry_ref = refs
    else:
        q_ref, kt_ref, v_ref, u_ref, o_ref, acc_ref, carry_ref = refs
    qi = pl.program_id(2)
    q = q_ref[...]
    lane = lax.broadcasted_iota(jnp.int32, (t, LANES), 1)
    head_of_lane = lane // HEAD_DIM
    q_heads = [jnp.where(head_of_lane == hh, q, jnp.zeros_like(q))
               for hh in range(HEADS_PER_GROUP)]
    acc_ref[...] = jnp.zeros_like(acc_ref)
    carry_ref[...] = jnp.zeros_like(carry_ref)

    def one_head(hh, kt_blk, v_blk, mask, tk, last):
        z = _dot(q_heads[hh], kt_blk)
        soft = jnp.log(1.0 + jnp.exp(-jnp.abs(z)))
        log_beta = jnp.minimum(z, 0.0) - soft
        log_1m = log_beta - z
        if mask is not None:
            log_1m = jnp.where(mask, log_1m, 0.0)
        width = tk if last else tk + LANES
        sums = _dot(log_1m.astype(BF16), u_ref[0:tk, 0:width])
        carry = carry_ref[hh]
        tail = sums[:, 0:tk] + jnp.concatenate([carry] * (tk // LANES), axis=1)
        w = jnp.exp(log_beta + tail)
        if mask is not None:
            w = jnp.where(mask, w, 0.0)
        acc_ref[hh] += _dot(w.astype(BF16), v_blk)
        if not last:
            carry_ref[hh] = carry + sums[:, tk:tk + LANES]

    def block(kt_blk, v_blk, mask, tk, last):
        for hh in range(HEADS_PER_GROUP):
            one_head(hh, kt_blk, v_blk, mask, tk, last)

    row_i = lax.broadcasted_iota(jnp.int32, (t, t), 0)
    col_i = lax.broadcasted_iota(jnp.int32, (t, t), 1)
    start = pl.multiple_of(qi * t, t)
    block(kt_ref[:, pl.ds(start, t)], v_ref[pl.ds(start, t), :],
          col_i < row_i, t, False)

    def body(j, c):
        off = pl.multiple_of((qi - 1 - j) * t, t)
        block(kt_ref[:, pl.ds(off, t)], v_ref[pl.ds(off, t), :], None, t, False)
        return c

    lax.fori_loop(0, qi, body, 0)

    if has_meta:
        col_m = lax.broadcasted_iota(jnp.int32, (t, META_ROWS), 1)
        block(ktm_ref[...], vm_ref[...], col_m < N_META, META_ROWS, True)

    out = acc_ref[0]
    for hh in range(1, HEADS_PER_GROUP):
        out = jnp.where(head_of_lane == hh, acc_ref[hh], out)
    o_ref[...] = out.astype(BF16)


def _suffix_sum_matrix(t):
    r = jnp.arange(t)[:, None]
    c = jnp.arange(t + LANES)[None, :]
    return jnp.where((c >= t) | (r > c), 1.0, 0.0).astype(BF16)


def _attention(qv, kt, *, t, meta_qv=None, meta_kt=None):
    b, s, _ = qv.shape
    d = D_MODEL
    has_meta = meta_qv is not None
    u = _suffix_sum_matrix(t)
    v_col0 = d // LANES
    args = [qv, kt, qv]
    specs = [pl.BlockSpec((None, t, LANES), lambda bi, g, qi: (bi, qi, g)),
             pl.BlockSpec((None, LANES, s), lambda bi, g, qi: (bi, g, 0)),
             pl.BlockSpec((None, s, LANES), lambda bi, g, qi: (bi, 0, v_col0 + g))]
    if has_meta:
        args += [meta_kt, meta_qv]
        specs += [pl.BlockSpec((LANES, META_ROWS), lambda bi, g, qi: (g, 0)),
                  pl.BlockSpec((META_ROWS, LANES), lambda bi, g, qi: (0, v_col0 + g))]
    args.append(u)
    specs.append(_resident(u.shape))
    kernel = functools.partial(_attn_kernel, t=t, has_meta=has_meta)
    return pl.pallas_call(
        kernel,
        grid=(b, N_GROUPS, s // t),
        in_specs=specs,
        out_specs=pl.BlockSpec((None, t, LANES), lambda bi, g, qi: (bi, qi, g)),
        out_shape=jax.ShapeDtypeStruct((b, s, d), BF16),
        scratch_shapes=[pltpu.VMEM((HEADS_PER_GROUP, t, LANES), F32),
                        pltpu.VMEM((HEADS_PER_GROUP, t, LANES), F32)],
        compiler_params=_params("arbitrary", "arbitrary", "arbitrary"),
        name="stick_breaking_attention",
    )(*args)


def kernel(x, meta_tokens, conv_norm, conv_w_in, conv_w, conv_w_out, attn_norm,
           attn_w_qkv, attn_w_out, mlp_norm, mlp_w1, mlp_w2, final_norm):
    b, s, d = x.shape
    depth = mlp_norm.shape[0]
    hr = x.reshape(b * s, d)
    hm = jnp.pad(meta_tokens.astype(x.dtype), ((0, META_ROWS - N_META), (0, 0)))
    zero_hist = jnp.zeros((SUBLANES, d), F32)
    row = lambda v: v.reshape(1, d)

    for i in range(depth):
        j = i // 2
        last = i == depth - 1
        w1 = mlp_w1[i].astype(BF16)
        w2 = mlp_w2[i].astype(BF16)
        if i % 2 == 0:
            w_in = conv_w_in[j].astype(BF16)
            w_out = conv_w_out[j].astype(BF16)
            g = row(conv_norm[j])
            hm, hist = _conv_mixer(hm, g, w_in, conv_w[j], w_out, zero_hist,
                                   tm=META_ROWS, tiles_per_seq=1, hist_row=N_META)
            hr = _conv_mixer(hr, g, w_in, conv_w[j], w_out, hist,
                             tm=ROW_TILE, tiles_per_seq=s // ROW_TILE)
            hm = _mlp(hm, row(mlp_norm[i]), w1, w2, tm=META_ROWS)
            hr = _mlp(hr, row(mlp_norm[i]), w1, w2, tm=ROW_TILE,
                      final_g=row(final_norm) if last else None)
        else:
            w_qkv = attn_w_qkv[j]
            w_qv = jnp.concatenate([w_qkv[:, :d], w_qkv[:, 2 * d:]], axis=1).astype(BF16)
            w_kt = w_qkv[:, d:2 * d].T.astype(BF16)
            w_o = attn_w_out[j].astype(BF16)
            g = row(attn_norm[j])
            qv_m, kt_m = _qkv(hm, g, w_qv, w_kt, tm=META_ROWS, seq=META_ROWS)
            qv_r, kt_r = _qkv(hr, g, w_qv, w_kt, tm=ROW_TILE, seq=s)
            o_r = _attention(qv_r.reshape(b, s, 2 * d), kt_r, t=ATTN_TILE,
                             meta_qv=qv_m, meta_kt=kt_m[0])
            if not last:
                o_m = _attention(qv_m.reshape(1, META_ROWS, 2 * d), kt_m, t=META_ROWS)
                hm = _mlp(hm, row(mlp_norm[i]), w1, w2, tm=META_ROWS,
                          attn=o_m.reshape(META_ROWS, d), w_o=w_o)
            hr = _mlp(hr, row(mlp_norm[i]), w1, w2, tm=ROW_TILE,
                      attn=o_r.reshape(b * s, d), w_o=w_o,
                      final_g=row(final_norm) if last else None)
    return hr.reshape(b, s, d)
```

```python
import functools
import math

import numpy as np
import jax
import jax.numpy as jnp
from jax import lax
from jax.experimental import pallas as pl
from jax.experimental.pallas import tpu as pltpu

D_MODEL = 1024
N_META = 16
N_HEADS = 16
HEAD_DIM = D_MODEL // N_HEADS
CONV_W = 3
D_FF = 4 * D_MODEL
RMS_EPS = 1e-6

LANES = 128
SUBLANES = 8
HEADS_PER_GROUP = LANES // HEAD_DIM
N_GROUPS = N_HEADS // HEADS_PER_GROUP
META_ROWS = LANES
ROW_TILE = 512
ATTN_TILE = 256
FF_CHUNK = 1024
VMEM_LIMIT_BYTES = 56 * 1024 * 1024
LOG2_E = math.log2(math.e)
MASK_BIAS = -1e30

BF16 = jnp.bfloat16
F32 = jnp.float32


def _dot(a, b):
    return jnp.dot(a, b, preferred_element_type=F32)


def _rmsnorm(x, g):
    return x * lax.rsqrt(jnp.mean(x * x, axis=-1, keepdims=True) + RMS_EPS) * g


def _resident(shape):
    zeros = (0,) * len(shape)
    return pl.BlockSpec(shape, lambda *_: zeros, pipeline_mode=pl.Buffered(1))


def _params(*semantics, flags=None):
    return pltpu.CompilerParams(dimension_semantics=semantics,
                                vmem_limit_bytes=VMEM_LIMIT_BYTES, flags=flags)


def _conv_mixer_kernel(x_ref, g_ref, w_in_ref, cw_ref, w_out_ref, hist_ref,
                       *rest, tm, tiles_per_seq, hist_row):
    if hist_row is None:
        o_ref, ubuf = rest
    else:
        o_ref, hist_out_ref, ubuf = rest
    i = pl.program_id(0)
    first = (i % tiles_per_seq) == 0

    @pl.when(first)
    def _():
        ubuf[0:SUBLANES, :] = hist_ref[...]

    @pl.when(jnp.logical_not(first))
    def _():
        ubuf[0:SUBLANES, :] = ubuf[tm:tm + SUBLANES, :]

    x = x_ref[...]
    h = _rmsnorm(x, g_ref[...]).astype(BF16)
    d = D_MODEL
    gate_c = _dot(h, w_in_ref[:, d:2 * d])
    val = _dot(h, w_in_ref[:, 2 * d:3 * d])
    ubuf[SUBLANES:SUBLANES + tm, :] = gate_c * val
    if hist_row is not None:
        hist_out_ref[...] = ubuf[hist_row:hist_row + SUBLANES, :]
    cw = cw_ref[...]
    conv = ubuf[SUBLANES:SUBLANES + tm, :] * cw[CONV_W - 1:CONV_W, :]
    for k in range(CONV_W - 1):
        back = CONV_W - 1 - k
        conv = conv + ubuf[SUBLANES - back:SUBLANES - back + tm, :] * cw[k:k + 1, :]
    gate_b = _dot(h, w_in_ref[:, 0:d])
    y = (gate_b * conv).astype(BF16)
    o_ref[...] = x + _dot(y, w_out_ref[...])


def _conv_mixer(x, g, w_in, cw, w_out, hist, *, tm, tiles_per_seq, hist_row=None):
    n, d = x.shape
    kernel = functools.partial(_conv_mixer_kernel, tm=tm,
                               tiles_per_seq=tiles_per_seq, hist_row=hist_row)
    row = pl.BlockSpec((tm, d), lambda i: (i, 0))
    out_shape = jax.ShapeDtypeStruct((n, d), F32)
    out_specs = row
    if hist_row is not None:
        out_shape = (out_shape, jax.ShapeDtypeStruct((SUBLANES, d), F32))
        out_specs = (row, pl.BlockSpec((SUBLANES, d), lambda i: (0, 0)))
    return pl.pallas_call(
        kernel,
        grid=(n // tm,),
        in_specs=[row, _resident((1, d)), _resident((d, 3 * d)),
                  _resident((CONV_W, d)), _resident((d, d)),
                  _resident((SUBLANES, d))],
        out_specs=out_specs,
        out_shape=out_shape,
        scratch_shapes=[pltpu.VMEM((tm + 2 * SUBLANES, d), F32)],
        compiler_params=_params("arbitrary"),
        name="conv_mixer",
    )(x, g, w_in, cw, w_out, hist)


def _mlp_kernel(*refs, with_proj, with_final_norm):
    refs = list(refs)
    x_ref = refs.pop(0)
    if with_proj:
        a_ref, w_o_ref = refs.pop(0), refs.pop(0)
    g_ref, w1_ref, w2_ref = refs.pop(0), refs.pop(0), refs.pop(0)
    if with_final_norm:
        gf_ref = refs.pop(0)
    (o_ref,) = refs

    x = x_ref[...]
    if with_proj:
        x = x + _dot(a_ref[...], w_o_ref[...])
    h = _rmsnorm(x, g_ref[...]).astype(BF16)
    acc = x
    for c in range(D_FF // FF_CHUNK):
        cols = slice(c * FF_CHUNK, (c + 1) * FF_CHUNK)
        a = jnp.maximum(_dot(h, w1_ref[:, cols]), 0.0)
        acc = acc + _dot((a * a).astype(BF16), w2_ref[cols, :])
    if with_final_norm:
        acc = _rmsnorm(acc, gf_ref[...])
    o_ref[...] = acc


def _mlp(x, g, w1, w2, *, tm, attn=None, w_o=None, final_g=None):
    n, d = x.shape
    with_proj = attn is not None
    with_final_norm = final_g is not None
    row = pl.BlockSpec((tm, d), lambda i: (i, 0))
    args, specs = [x], [row]
    if with_proj:
        args += [attn, w_o]
        specs += [row, _resident((d, d))]
    args += [g, w1, w2]
    specs += [_resident((1, d)), _resident((d, D_FF)), _resident((D_FF, d))]
    if with_final_norm:
        args.append(final_g)
        specs.append(_resident((1, d)))
    kernel = functools.partial(_mlp_kernel, with_proj=with_proj,
                               with_final_norm=with_final_norm)
    return pl.pallas_call(
        kernel,
        grid=(n // tm,),
        in_specs=specs,
        out_specs=row,
        out_shape=jax.ShapeDtypeStruct((n, d), F32),
        compiler_params=_params("arbitrary"),
        name="mlp",
    )(*args)


def _qkv_kernel(x_ref, g_ref, w_qv_ref, w_kt_ref, qv_ref, kt_ref):
    d = D_MODEL
    h = _rmsnorm(x_ref[...], g_ref[...]).astype(BF16)
    scale = 1.0 / math.sqrt(HEAD_DIM)
    qv_ref[:, 0:d] = (_dot(h, w_qv_ref[:, 0:d]) * scale).astype(BF16)
    qv_ref[:, d:2 * d] = _dot(h, w_qv_ref[:, d:2 * d]).astype(BF16)
    kt = lax.dot_general(w_kt_ref[...], h, (((1,), (1,)), ((), ())),
                         preferred_element_type=F32)
    kt_ref[...] = kt.astype(BF16)


def _qkv(x, g, w_qv, w_kt, *, tm, seq):
    n, d = x.shape
    tiles_per_seq = seq // tm
    return pl.pallas_call(
        _qkv_kernel,
        grid=(n // tm,),
        in_specs=[pl.BlockSpec((tm, d), lambda i: (i, 0)), _resident((1, d)),
                  _resident((d, 2 * d)), _resident((d, d))],
        out_specs=(pl.BlockSpec((tm, 2 * d), lambda i: (i, 0)),
                   pl.BlockSpec((None, d, tm),
                                lambda i: (i // tiles_per_seq, 0, i % tiles_per_seq))),
        out_shape=(jax.ShapeDtypeStruct((n, 2 * d), BF16),
                   jax.ShapeDtypeStruct((n // seq, d, seq), BF16)),
        compiler_params=_params("arbitrary"),
        name="qkv_proj",
    )(x, g, w_qv, w_kt)


def _suffix_sum_matrix(t, sign):
    r = jnp.arange(t)[:, None]
    c = jnp.arange(t + LANES)[None, :]
    return jnp.where((c >= t) | (r > c), sign, 0.0).astype(BF16)


def _neg_abs(x):
    bits = lax.bitcast_convert_type(x, jnp.uint32) | jnp.uint32(0x80000000)
    return lax.bitcast_convert_type(bits, F32)


def _head_of_lane(shape):
    return lax.broadcasted_iota(jnp.int32, shape, len(shape) - 1) // HEAD_DIM


def _attn_small_kernel(q_ref, kt_ref, v_ref, u_ref, o_ref, *, t):
    q = q_ref[...]
    head_of_lane = _head_of_lane((t, LANES))
    row_i = lax.broadcasted_iota(jnp.int32, (t, t), 0)
    col_i = lax.broadcasted_iota(jnp.int32, (t, t), 1)
    causal = col_i < row_i
    out = None
    for hh in range(HEADS_PER_GROUP):
        qh = jnp.where(head_of_lane == hh, q, jnp.zeros_like(q))
        z = _dot(qh, kt_ref[...])
        soft = jnp.log(1.0 + jnp.exp(-jnp.abs(z)))
        log_beta = jnp.minimum(z, 0.0) - soft
        log_1m = jnp.where(causal, log_beta - z, 0.0)
        tail = _dot(log_1m.astype(BF16), u_ref[:, 0:t])
        w = jnp.where(causal, jnp.exp(log_beta + tail), 0.0)
        oh = _dot(w.astype(BF16), v_ref[...])
        out = oh if out is None else jnp.where(head_of_lane == hh, oh, out)
    o_ref[...] = out.astype(BF16)


def _attention_small(qv, kt):
    t = qv.shape[0]
    d = D_MODEL
    u = _suffix_sum_matrix(t, 1.0)
    v_col0 = d // LANES
    return pl.pallas_call(
        functools.partial(_attn_small_kernel, t=t),
        grid=(N_GROUPS,),
        in_specs=[pl.BlockSpec((t, LANES), lambda g: (0, g)),
                  pl.BlockSpec((LANES, t), lambda g: (g, 0)),
                  pl.BlockSpec((t, LANES), lambda g: (0, v_col0 + g)),
                  _resident(u.shape)],
        out_specs=pl.BlockSpec((t, LANES), lambda g: (0, g)),
        out_shape=jax.ShapeDtypeStruct((t, d), BF16),
        compiler_params=_params("arbitrary"),
        name="attention_meta",
    )(qv, kt, qv, u)


_KIND_PLAIN, _KIND_DIAG, _KIND_META = 0, 1, 2
_PIPE_DEPTH = 4
_UNROLL = 8
_ITEM_FIELDS = 4


def _attention_items(s, t):
    idle = [s, 0, _KIND_PLAIN, 1]
    items = []
    for qi in range(s // t):
        items.append([qi * t, qi * t, _KIND_DIAG, 1])
        for j in range(qi - 1, -1, -1):
            items.append([qi * t, j * t, _KIND_PLAIN, 0])
        items.append([qi * t, s, _KIND_META, 0])
    lead = _PIPE_DEPTH - 1
    n_iter = len(items) + lead
    n_iter += -n_iter % _UNROLL
    table = [idle] * lead + items
    table += [idle] * (n_iter + lead - len(table))
    return np.asarray(table, np.int32).reshape(-1), n_iter


def _mask_bias(t):
    r = np.arange(t)[:, None]
    c = np.arange(t)[None, :]
    bias = np.zeros((3, t, t), np.float32)
    bias[_KIND_DIAG] = np.where(c < r, 0.0, MASK_BIAS)
    bias[_KIND_META] = np.where(c < N_META, 0.0, MASK_BIAS) + 0.0 * r
    return jnp.asarray(bias)


def _attn_pipe_kernel(tab_ref, q_ref, kt_ref, v_ref, ktm_ref, vm_ref, u_ref, bias_ref,
                      o_ref, q_all, kt_all, v_all, zbuf, lbuf, pbuf, wbuf, carry_ref,
                      acc_ref, obuf, *, s, t, n_iter):
    nh = HEADS_PER_GROUP
    first_step = jnp.logical_and(pl.program_id(0) == 0, pl.program_id(1) == 0)

    @pl.when(first_step)
    def _():
        for ref in (zbuf, lbuf, pbuf, wbuf, carry_ref, acc_ref):
            ref[...] = jnp.zeros_like(ref)

    q = q_ref[...]
    v = v_ref[...]
    vm = vm_ref[...]
    for hh in range(nh):
        q_all[hh, 0:s, :] = jnp.where(_head_of_lane(q.shape) == hh, q, jnp.zeros_like(q))
        q_all[hh, s:s + t, :] = jnp.zeros((t, LANES), BF16)
        v_all[hh, 0:s, :] = jnp.where(_head_of_lane(v.shape) == hh, v, jnp.zeros_like(v))
        v_all[hh, s:s + META_ROWS, :] = jnp.where(_head_of_lane(vm.shape) == hh, vm,
                                                  jnp.zeros_like(vm))
        v_all[hh, s + META_ROWS:s + t, :] = jnp.zeros((t - META_ROWS, LANES), BF16)
    kt_all[:, 0:s] = kt_ref[...]
    kt_all[:, s:s + META_ROWS] = ktm_ref[...]
    kt_all[:, s + META_ROWS:s + t] = jnp.zeros((LANES, t - META_ROWS), BF16)

    def field(item, f):
        return tab_ref[item * _ITEM_FIELDS + f]

    def stage_scores(item, p):
        q_off = pl.multiple_of(field(item, 0), t)
        k_off = pl.multiple_of(field(item, 1), t)
        kt_blk = kt_all[:, pl.ds(k_off, t)]
        for hh in range(nh):
            zbuf[p, hh] = _dot(q_all[hh, pl.ds(q_off, t), :], kt_blk)

    def stage_logs(item, p):
        bias = bias_ref[field(item, 2)]
        for hh in range(nh):
            zs = zbuf[1 - p, hh] * LOG2_E + bias
            soft = jnp.log(1.0 + jnp.exp2(_neg_abs(zs))) * LOG2_E
            neg_log_1m = jnp.maximum(zs, 0.0) + soft
            lbuf[p, hh] = zs - neg_log_1m
            pbuf[p, hh] = neg_log_1m.astype(BF16)

    def stage_weights(item, p):
        first = field(item, 3) != 0
        for hh in range(nh):
            sums = _dot(pbuf[1 - p, hh], u_ref[...])
            carry = jnp.where(first, 0.0, carry_ref[hh])
            tail = sums[:, 0:t] + jnp.concatenate([carry] * (t // LANES), axis=1)
            wbuf[p, hh] = jnp.exp2(lbuf[1 - p, hh] + tail).astype(BF16)
            carry_ref[hh] = carry + sums[:, t:t + LANES]

    def stage_output(item, p):
        q_off = pl.multiple_of(field(item, 0), t)
        k_off = pl.multiple_of(field(item, 1), t)
        first = field(item, 3) != 0
        w_cat = jnp.concatenate([wbuf[1 - p, hh] for hh in range(nh)], axis=1)
        v_cat = jnp.concatenate([v_all[hh, pl.ds(k_off, t), :] for hh in range(nh)], axis=0)
        acc = jnp.where(first, 0.0, acc_ref[...]) + _dot(w_cat, v_cat)
        acc_ref[...] = acc
        obuf[pl.ds(q_off, t), :] = acc.astype(BF16)

    def iteration(i, p):
        stage_logs(i + 2, p)
        stage_weights(i + 1, p)
        stage_scores(i + 3, p)
        stage_output(i, p)

    def body(k, c):
        for j in range(_UNROLL):
            iteration(_UNROLL * k + j, j % 2)
        return c

    lax.fori_loop(0, n_iter // _UNROLL, body, 0)
    o_ref[...] = obuf[0:s, :]


def _attention(qv, kt, meta_qv, meta_kt, *, t):
    b, s, _ = qv.shape
    d = D_MODEL
    nh = HEADS_PER_GROUP
    table, n_iter = _attention_items(s, t)
    u = _suffix_sum_matrix(t, -1.0)
    bias = _mask_bias(t)
    v_col0 = d // LANES
    grid_spec = pltpu.PrefetchScalarGridSpec(
        num_scalar_prefetch=1,
        grid=(b, N_GROUPS),
        in_specs=[
            pl.BlockSpec((None, s, LANES), lambda bi, g, tab: (bi, 0, g)),
            pl.BlockSpec((None, LANES, s), lambda bi, g, tab: (bi, g, 0)),
            pl.BlockSpec((None, s, LANES), lambda bi, g, tab: (bi, 0, v_col0 + g)),
            pl.BlockSpec((LANES, META_ROWS), lambda bi, g, tab: (g, 0)),
            pl.BlockSpec((META_ROWS, LANES), lambda bi, g, tab: (0, v_col0 + g)),
            pl.BlockSpec(u.shape, lambda bi, g, tab: (0, 0)),
            pl.BlockSpec(bias.shape, lambda bi, g, tab: (0, 0, 0)),
        ],
        out_specs=pl.BlockSpec((None, s, LANES), lambda bi, g, tab: (bi, 0, g)),
        scratch_shapes=[
            pltpu.VMEM((nh, s + t, LANES), BF16),
            pltpu.VMEM((LANES, s + t), BF16),
            pltpu.VMEM((nh, s + t, LANES), BF16),
            pltpu.VMEM((2, nh, t, t), F32),
            pltpu.VMEM((2, nh, t, t), F32),
            pltpu.VMEM((2, nh, t, t), BF16),
            pltpu.VMEM((2, nh, t, t), BF16),
            pltpu.VMEM((nh, t, LANES), F32),
            pltpu.VMEM((t, LANES), F32),
            pltpu.VMEM((s + t, LANES), BF16),
        ],
    )
    kernel = functools.partial(_attn_pipe_kernel, s=s, t=t, n_iter=n_iter)
    return pl.pallas_call(
        kernel,
        grid_spec=grid_spec,
        out_shape=jax.ShapeDtypeStruct((b, s, d), BF16),
        compiler_params=_params("arbitrary", "arbitrary"),
        name="stick_breaking_attention",
    )(jnp.asarray(table), qv, kt, qv, meta_kt, meta_qv, u, bias)


def kernel(x, meta_tokens, conv_norm, conv_w_in, conv_w, conv_w_out, attn_norm,
           attn_w_qkv, attn_w_out, mlp_norm, mlp_w1, mlp_w2, final_norm):
    b, s, d = x.shape
    depth = mlp_norm.shape[0]
    hr = x.reshape(b * s, d)
    hm = jnp.pad(meta_tokens.astype(x.dtype), ((0, META_ROWS - N_META), (0, 0)))
    zero_hist = jnp.zeros((SUBLANES, d), F32)
    row = lambda v: v.reshape(1, d)

    for i in range(depth):
        j = i // 2
        last = i == depth - 1
        w1 = mlp_w1[i].astype(BF16)
        w2 = mlp_w2[i].astype(BF16)
        if i % 2 == 0:
            w_in = conv_w_in[j].astype(BF16)
            w_out = conv_w_out[j].astype(BF16)
            g = row(conv_norm[j])
            hm, hist = _conv_mixer(hm, g, w_in, conv_w[j], w_out, zero_hist,
                                   tm=META_ROWS, tiles_per_seq=1, hist_row=N_META)
            hr = _conv_mixer(hr, g, w_in, conv_w[j], w_out, hist,
                             tm=ROW_TILE, tiles_per_seq=s // ROW_TILE)
            hm = _mlp(hm, row(mlp_norm[i]), w1, w2, tm=META_ROWS)
            hr = _mlp(hr, row(mlp_norm[i]), w1, w2, tm=ROW_TILE,
                      final_g=row(final_norm) if last else None)
        else:
            w_qkv = attn_w_qkv[j]
            w_qv = jnp.concatenate([w_qkv[:, :d], w_qkv[:, 2 * d:]], axis=1).astype(BF16)
            w_kt = w_qkv[:, d:2 * d].T.astype(BF16)
            w_o = attn_w_out[j].astype(BF16)
            g = row(attn_norm[j])
            qv_m, kt_m = _qkv(hm, g, w_qv, w_kt, tm=META_ROWS, seq=META_ROWS)
            qv_r, kt_r = _qkv(hr, g, w_qv, w_kt, tm=ROW_TILE, seq=s)
            o_r = _attention(qv_r.reshape(b, s, 2 * d), kt_r, qv_m, kt_m[0], t=ATTN_TILE)
            if not last:
                o_m = _attention_small(qv_m, kt_m[0])
                hm = _mlp(hm, row(mlp_norm[i]), w1, w2, tm=META_ROWS, attn=o_m, w_o=w_o)
            hr = _mlp(hr, row(mlp_norm[i]), w1, w2, tm=ROW_TILE,
                      attn=o_r.reshape(b * s, d), w_o=w_o,
                      final_g=row(final_norm) if last else None)
    return hr.reshape(b, s, d)
```

```python
import functools
import math

import numpy as np
import jax
import jax.numpy as jnp
from jax import lax
from jax.experimental import pallas as pl
from jax.experimental.pallas import tpu as pltpu

D_MODEL = 1024
N_META = 16
N_HEADS = 16
HEAD_DIM = D_MODEL // N_HEADS
CONV_W = 3
D_FF = 4 * D_MODEL
RMS_EPS = 1e-6

LANES = 128
SUBLANES = 8
HEADS_PER_GROUP = LANES // HEAD_DIM
N_GROUPS = N_HEADS // HEADS_PER_GROUP
META_ROWS = LANES
ROW_TILE = 512
ATTN_TILE = 256
FF_CHUNK = 1024
VMEM_LIMIT_BYTES = 56 * 1024 * 1024
LOG2_E = math.log2(math.e)
MASK_BIAS = -1e30

BF16 = jnp.bfloat16
F32 = jnp.float32


def _dot(a, b):
    return jnp.dot(a, b, preferred_element_type=F32)


def _rmsnorm(x, g):
    return x * lax.rsqrt(jnp.mean(x * x, axis=-1, keepdims=True) + RMS_EPS) * g


def _resident(shape):
    zeros = (0,) * len(shape)
    return pl.BlockSpec(shape, lambda *_: zeros, pipeline_mode=pl.Buffered(1))


def _params(*semantics, flags=None):
    return pltpu.CompilerParams(dimension_semantics=semantics,
                                vmem_limit_bytes=VMEM_LIMIT_BYTES, flags=flags)


def _conv_mixer_kernel(x_ref, g_ref, w_in_ref, cw_ref, w_out_ref, hist_ref,
                       *rest, tm, tiles_per_seq, hist_row):
    if hist_row is None:
        o_ref, ubuf = rest
    else:
        o_ref, hist_out_ref, ubuf = rest
    i = pl.program_id(0)
    first = (i % tiles_per_seq) == 0

    @pl.when(first)
    def _():
        ubuf[0:SUBLANES, :] = hist_ref[...]

    @pl.when(jnp.logical_not(first))
    def _():
        ubuf[0:SUBLANES, :] = ubuf[tm:tm + SUBLANES, :]

    x = x_ref[...]
    h = _rmsnorm(x, g_ref[...]).astype(BF16)
    d = D_MODEL
    gate_c = _dot(h, w_in_ref[:, d:2 * d])
    val = _dot(h, w_in_ref[:, 2 * d:3 * d])
    ubuf[SUBLANES:SUBLANES + tm, :] = gate_c * val
    if hist_row is not None:
        hist_out_ref[...] = ubuf[hist_row:hist_row + SUBLANES, :]
    cw = cw_ref[...]
    conv = ubuf[SUBLANES:SUBLANES + tm, :] * cw[CONV_W - 1:CONV_W, :]
    for k in range(CONV_W - 1):
        back = CONV_W - 1 - k
        conv = conv + ubuf[SUBLANES - back:SUBLANES - back + tm, :] * cw[k:k + 1, :]
    gate_b = _dot(h, w_in_ref[:, 0:d])
    y = (gate_b * conv).astype(BF16)
    o_ref[...] = x + _dot(y, w_out_ref[...])


def _conv_mixer(x, g, w_in, cw, w_out, hist, *, tm, tiles_per_seq, hist_row=None):
    n, d = x.shape
    kernel = functools.partial(_conv_mixer_kernel, tm=tm,
                               tiles_per_seq=tiles_per_seq, hist_row=hist_row)
    row = pl.BlockSpec((tm, d), lambda i: (i, 0))
    out_shape = jax.ShapeDtypeStruct((n, d), F32)
    out_specs = row
    if hist_row is not None:
        out_shape = (out_shape, jax.ShapeDtypeStruct((SUBLANES, d), F32))
        out_specs = (row, pl.BlockSpec((SUBLANES, d), lambda i: (0, 0)))
    return pl.pallas_call(
        kernel,
        grid=(n // tm,),
        in_specs=[row, _resident((1, d)), _resident((d, 3 * d)),
                  _resident((CONV_W, d)), _resident((d, d)),
                  _resident((SUBLANES, d))],
        out_specs=out_specs,
        out_shape=out_shape,
        scratch_shapes=[pltpu.VMEM((tm + 2 * SUBLANES, d), F32)],
        compiler_params=_params("arbitrary"),
        name="conv_mixer",
    )(x, g, w_in, cw, w_out, hist)


def _mlp_kernel(*refs, with_proj, with_final_norm):
    refs = list(refs)
    x_ref = refs.pop(0)
    if with_proj:
        a_ref, w_o_ref = refs.pop(0), refs.pop(0)
    g_ref, w1_ref, w2_ref = refs.pop(0), refs.pop(0), refs.pop(0)
    if with_final_norm:
        gf_ref = refs.pop(0)
    (o_ref,) = refs

    x = x_ref[...]
    if with_proj:
        x = x + _dot(a_ref[...], w_o_ref[...])
    h = _rmsnorm(x, g_ref[...]).astype(BF16)
    acc = x
    for c in range(D_FF // FF_CHUNK):
        cols = slice(c * FF_CHUNK, (c + 1) * FF_CHUNK)
        a = jnp.maximum(_dot(h, w1_ref[:, cols]), 0.0)
        acc = acc + _dot((a * a).astype(BF16), w2_ref[cols, :])
    if with_final_norm:
        acc = _rmsnorm(acc, gf_ref[...])
    o_ref[...] = acc


def _mlp(x, g, w1, w2, *, tm, attn=None, w_o=None, final_g=None):
    n, d = x.shape
    with_proj = attn is not None
    with_final_norm = final_g is not None
    row = pl.BlockSpec((tm, d), lambda i: (i, 0))
    args, specs = [x], [row]
    if with_proj:
        args += [attn, w_o]
        specs += [row, _resident((d, d))]
    args += [g, w1, w2]
    specs += [_resident((1, d)), _resident((d, D_FF)), _resident((D_FF, d))]
    if with_final_norm:
        args.append(final_g)
        specs.append(_resident((1, d)))
    kernel = functools.partial(_mlp_kernel, with_proj=with_proj,
                               with_final_norm=with_final_norm)
    return pl.pallas_call(
        kernel,
        grid=(n // tm,),
        in_specs=specs,
        out_specs=row,
        out_shape=jax.ShapeDtypeStruct((n, d), F32),
        compiler_params=_params("arbitrary"),
        name="mlp",
    )(*args)


def _qkv_kernel(x_ref, g_ref, w_qv_ref, w_kt_ref, qv_ref, kt_ref):
    d = D_MODEL
    h = _rmsnorm(x_ref[...], g_ref[...]).astype(BF16)
    scale = 1.0 / math.sqrt(HEAD_DIM)
    qv_ref[:, 0:d] = (_dot(h, w_qv_ref[:, 0:d]) * scale).astype(BF16)
    qv_ref[:, d:2 * d] = _dot(h, w_qv_ref[:, d:2 * d]).astype(BF16)
    kt = lax.dot_general(w_kt_ref[...], h, (((1,), (1,)), ((), ())),
                         preferred_element_type=F32)
    kt_ref[...] = kt.astype(BF16)


def _qkv(x, g, w_qv, w_kt, *, tm, seq):
    n, d = x.shape
    tiles_per_seq = seq // tm
    return pl.pallas_call(
        _qkv_kernel,
        grid=(n // tm,),
        in_specs=[pl.BlockSpec((tm, d), lambda i: (i, 0)), _resident((1, d)),
                  _resident((d, 2 * d)), _resident((d, d))],
        out_specs=(pl.BlockSpec((tm, 2 * d), lambda i: (i, 0)),
                   pl.BlockSpec((None, d, tm),
                                lambda i: (i // tiles_per_seq, 0, i % tiles_per_seq))),
        out_shape=(jax.ShapeDtypeStruct((n, 2 * d), BF16),
                   jax.ShapeDtypeStruct((n // seq, d, seq), BF16)),
        compiler_params=_params("arbitrary"),
        name="qkv_proj",
    )(x, g, w_qv, w_kt)


def _suffix_sum_matrix(t, sign):
    r = jnp.arange(t)[:, None]
    c = jnp.arange(t + LANES)[None, :]
    return jnp.where((c >= t) | (r > c), sign, 0.0).astype(BF16)


def _head_of_lane(shape):
    return lax.broadcasted_iota(jnp.int32, shape, len(shape) - 1) // HEAD_DIM


def _attn_small_kernel(q_ref, kt_ref, v_ref, u_ref, o_ref, *, t):
    q = q_ref[...]
    head_of_lane = _head_of_lane((t, LANES))
    row_i = lax.broadcasted_iota(jnp.int32, (t, t), 0)
    col_i = lax.broadcasted_iota(jnp.int32, (t, t), 1)
    causal = col_i < row_i
    out = None
    for hh in range(HEADS_PER_GROUP):
        qh = jnp.where(head_of_lane == hh, q, jnp.zeros_like(q))
        z = _dot(qh, kt_ref[...])
        soft = jnp.log(1.0 + jnp.exp(-jnp.abs(z)))
        log_beta = jnp.minimum(z, 0.0) - soft
        log_1m = jnp.where(causal, log_beta - z, 0.0)
        tail = _dot(log_1m.astype(BF16), u_ref[:, 0:t])
        w = jnp.where(causal, jnp.exp(log_beta + tail), 0.0)
        oh = _dot(w.astype(BF16), v_ref[...])
        out = oh if out is None else jnp.where(head_of_lane == hh, oh, out)
    o_ref[...] = out.astype(BF16)


def _attention_small(qv, kt):
    t = qv.shape[0]
    d = D_MODEL
    u = _suffix_sum_matrix(t, 1.0)
    v_col0 = d // LANES
    return pl.pallas_call(
        functools.partial(_attn_small_kernel, t=t),
        grid=(N_GROUPS,),
        in_specs=[pl.BlockSpec((t, LANES), lambda g: (0, g)),
                  pl.BlockSpec((LANES, t), lambda g: (g, 0)),
                  pl.BlockSpec((t, LANES), lambda g: (0, v_col0 + g)),
                  _resident(u.shape)],
        out_specs=pl.BlockSpec((t, LANES), lambda g: (0, g)),
        out_shape=jax.ShapeDtypeStruct((t, d), BF16),
        compiler_params=_params("arbitrary"),
        name="attention_meta",
    )(qv, kt, qv, u)


_KIND_PLAIN, _KIND_DIAG, _KIND_META = 0, 1, 2
_PIPE_DEPTH = 4
_UNROLL = 4
_ITEM_FIELDS = 5
_PIPELINED_BLOCKS = 2
LOG_UNDERFLOW = -105.0


def _tile_blocks(qi, s, t):
    blocks = [(qi * t, _KIND_DIAG)]
    blocks += [(j * t, _KIND_PLAIN) for j in range(qi - 1, -1, -1)]
    blocks.append((s, _KIND_META))
    return blocks


def _attention_items(s, t):
    idle = [s, 0, _KIND_PLAIN, 1, s // t]
    items = []
    for qi in range(s // t):
        for n, (k_off, kind) in enumerate(_tile_blocks(qi, s, t)[:_PIPELINED_BLOCKS]):
            items.append([qi * t, k_off, kind, int(n == 0), qi])
    lead = _PIPE_DEPTH - 1
    n_iter = len(items) + lead
    n_iter += -n_iter % _UNROLL
    table = [idle] * lead + items
    table += [idle] * (n_iter + lead - len(table))
    return np.asarray(table, np.int32).reshape(-1), n_iter


def _mask_bias(t):
    r = np.arange(t)[:, None]
    c = np.arange(t)[None, :]
    bias = np.zeros((3, t, t), np.float32)
    bias[_KIND_DIAG] = np.where(c < r, 0.0, MASK_BIAS)
    bias[_KIND_META] = np.where(c < N_META, 0.0, MASK_BIAS) + 0.0 * r
    return jnp.asarray(bias)


def _attn_pipe_kernel(tab_ref, q_ref, kt_ref, v_ref, ktm_ref, vm_ref, u_ref, bias_ref,
                      o_ref, q_all, kt_all, v_all, zbuf, lbuf, pbuf, wbuf, carry_ref,
                      acc_ref, carry_all, acc_all, obuf, *, s, t, n_iter):
    nh = HEADS_PER_GROUP
    first_step = jnp.logical_and(pl.program_id(0) == 0, pl.program_id(1) == 0)

    @pl.when(first_step)
    def _():
        for ref in (zbuf, lbuf, pbuf, wbuf, carry_ref, acc_ref):
            ref[...] = jnp.zeros_like(ref)

    q = q_ref[...]
    v = v_ref[...]
    vm = vm_ref[...]
    for hh in range(nh):
        q_all[hh, 0:s, :] = jnp.where(_head_of_lane(q.shape) == hh, q, jnp.zeros_like(q))
        q_all[hh, s:s + t, :] = jnp.zeros((t, LANES), BF16)
        v_all[hh, 0:s, :] = jnp.where(_head_of_lane(v.shape) == hh, v, jnp.zeros_like(v))
        v_all[hh, s:s + META_ROWS, :] = jnp.where(_head_of_lane(vm.shape) == hh, vm,
                                                  jnp.zeros_like(vm))
        v_all[hh, s + META_ROWS:s + t, :] = jnp.zeros((t - META_ROWS, LANES), BF16)
    kt_all[:, 0:s] = kt_ref[...]
    kt_all[:, s:s + META_ROWS] = ktm_ref[...]
    kt_all[:, s + META_ROWS:s + t] = jnp.zeros((LANES, t - META_ROWS), BF16)

    def stage_scores(q_off, k_off, p):
        kt_blk = kt_all[:, pl.ds(pl.multiple_of(k_off, t), t)]
        for hh in range(nh):
            zbuf[p, hh] = _dot(q_all[hh, pl.ds(pl.multiple_of(q_off, t), t), :], kt_blk)

    def stage_logs(kind, p):
        bias = bias_ref[kind]
        for hh in range(nh):
            z = zbuf[1 - p, hh] + bias
            soft = jnp.log(1.0 + jnp.exp2(jnp.abs(z) * (-LOG2_E)))
            neg_log_1m = jnp.maximum(z, 0.0) + soft
            lbuf[p, hh] = z - neg_log_1m
            pbuf[p, hh] = neg_log_1m.astype(BF16)

    def stage_weights(first, p):
        for hh in range(nh):
            sums = _dot(pbuf[1 - p, hh], u_ref[...])
            carry = carry_ref[hh] if first is None else jnp.where(first, 0.0, carry_ref[hh])
            tail = sums[:, 0:t] + jnp.concatenate([carry] * (t // LANES), axis=1)
            wbuf[p, hh] = jnp.exp(lbuf[1 - p, hh] + tail).astype(BF16)
            carry_ref[hh] = carry + sums[:, t:t + LANES]

    def stage_output(q_off, k_off, first, p):
        k_off = pl.multiple_of(k_off, t)
        w_cat = jnp.concatenate([wbuf[1 - p, hh] for hh in range(nh)], axis=1)
        v_cat = jnp.concatenate([v_all[hh, pl.ds(k_off, t), :] for hh in range(nh)], axis=0)
        acc = acc_ref[...] if first is None else jnp.where(first, 0.0, acc_ref[...])
        acc = acc + _dot(w_cat, v_cat)
        acc_ref[...] = acc
        obuf[pl.ds(pl.multiple_of(q_off, t), t), :] = acc.astype(BF16)

    def field(item, f):
        return tab_ref[item * _ITEM_FIELDS + f]

    def iteration(i, p):
        stage_logs(field(i + 2, 2), p)
        stage_weights(field(i + 1, 3) != 0, p)
        tile = field(i + 1, 4)
        carry_all[tile] = carry_ref[...]
        stage_scores(field(i + 3, 0), field(i + 3, 1), p)
        stage_output(field(i, 0), field(i, 1), field(i, 3) != 0, p)
        acc_all[field(i, 4)] = acc_ref[...]

    def body(k, c):
        for j in range(_UNROLL):
            iteration(_UNROLL * k + j, j % 2)
        return c

    lax.fori_loop(0, n_iter // _UNROLL, body, 0)

    def saturated():
        return jnp.max(carry_ref[...]) <= LOG_UNDERFLOW

    def finish_tile(qi, c):
        n_blocks = qi + 2
        carry_ref[...] = carry_all[qi]

        @pl.when(jnp.logical_not(saturated()))
        def _():
            acc_ref[...] = acc_all[qi]
            q_off = qi * t

            def more(state):
                n, done = state
                return jnp.logical_and(n < n_blocks, jnp.logical_not(done))

            def one_block(state):
                n, _ = state
                is_meta = n == n_blocks - 1
                k_off = jnp.where(is_meta, s, (qi - n) * t)
                kind = jnp.where(is_meta, _KIND_META, _KIND_PLAIN)
                stage_scores(q_off, k_off, 0)
                stage_logs(kind, 1)
                stage_weights(None, 0)
                stage_output(q_off, k_off, None, 1)
                return n + 1, saturated()

            lax.while_loop(more, one_block, (jnp.int32(_PIPELINED_BLOCKS), jnp.bool_(False)))

        return c

    first_open = _PIPELINED_BLOCKS - 1

    @pl.when(jnp.max(carry_all[first_open:s // t]) > LOG_UNDERFLOW)
    def _():
        lax.fori_loop(first_open, s // t, finish_tile, 0)

    o_ref[...] = obuf[0:s, :]


def _attention(qv, kt, meta_qv, meta_kt, *, t):
    b, s, _ = qv.shape
    d = D_MODEL
    nh = HEADS_PER_GROUP
    table, n_iter = _attention_items(s, t)
    u = _suffix_sum_matrix(t, -1.0)
    bias = _mask_bias(t)
    v_col0 = d // LANES
    grid_spec = pltpu.PrefetchScalarGridSpec(
        num_scalar_prefetch=1,
        grid=(b, N_GROUPS),
        in_specs=[
            pl.BlockSpec((None, s, LANES), lambda bi, g, tab: (bi, 0, g)),
            pl.BlockSpec((None, LANES, s), lambda bi, g, tab: (bi, g, 0)),
            pl.BlockSpec((None, s, LANES), lambda bi, g, tab: (bi, 0, v_col0 + g)),
            pl.BlockSpec((LANES, META_ROWS), lambda bi, g, tab: (g, 0)),
            pl.BlockSpec((META_ROWS, LANES), lambda bi, g, tab: (0, v_col0 + g)),
            pl.BlockSpec(u.shape, lambda bi, g, tab: (0, 0)),
            pl.BlockSpec(bias.shape, lambda bi, g, tab: (0, 0, 0)),
        ],
        out_specs=pl.BlockSpec((None, s, LANES), lambda bi, g, tab: (bi, 0, g)),
        scratch_shapes=[
            pltpu.VMEM((nh, s + t, LANES), BF16),
            pltpu.VMEM((LANES, s + t), BF16),
            pltpu.VMEM((nh, s + t, LANES), BF16),
            pltpu.VMEM((2, nh, t, t), F32),
            pltpu.VMEM((2, nh, t, t), F32),
            pltpu.VMEM((2, nh, t, t), BF16),
            pltpu.VMEM((2, nh, t, t), BF16),
            pltpu.VMEM((nh, t, LANES), F32),
            pltpu.VMEM((t, LANES), F32),
            pltpu.VMEM((s // t + 1, nh, t, LANES), F32),
            pltpu.VMEM((s // t + 1, t, LANES), F32),
            pltpu.VMEM((s + t, LANES), BF16),
        ],
    )
    kernel = functools.partial(_attn_pipe_kernel, s=s, t=t, n_iter=n_iter)
    return pl.pallas_call(
        kernel,
        grid_spec=grid_spec,
        out_shape=jax.ShapeDtypeStruct((b, s, d), BF16),
        compiler_params=_params("arbitrary", "arbitrary"),
        name="stick_breaking_attention",
    )(jnp.asarray(table), qv, kt, qv, meta_kt, meta_qv, u, bias)


def kernel(x, meta_tokens, conv_norm, conv_w_in, conv_w, conv_w_out, attn_norm,
           attn_w_qkv, attn_w_out, mlp_norm, mlp_w1, mlp_w2, final_norm):
    b, s, d = x.shape
    depth = mlp_norm.shape[0]
    hr = x.reshape(b * s, d)
    hm = jnp.pad(meta_tokens.astype(x.dtype), ((0, META_ROWS - N_META), (0, 0)))
    zero_hist = jnp.zeros((SUBLANES, d), F32)
    row = lambda v: v.reshape(1, d)

    for i in range(depth):
        j = i // 2
        last = i == depth - 1
        w1 = mlp_w1[i].astype(BF16)
        w2 = mlp_w2[i].astype(BF16)
        if i % 2 == 0:
            w_in = conv_w_in[j].astype(BF16)
            w_out = conv_w_out[j].astype(BF16)
            g = row(conv_norm[j])
            hm, hist = _conv_mixer(hm, g, w_in, conv_w[j], w_out, zero_hist,
                                   tm=META_ROWS, tiles_per_seq=1, hist_row=N_META)
            hr = _conv_mixer(hr, g, w_in, conv_w[j], w_out, hist,
                             tm=ROW_TILE, tiles_per_seq=s // ROW_TILE)
            hm = _mlp(hm, row(mlp_norm[i]), w1, w2, tm=META_ROWS)
            hr = _mlp(hr, row(mlp_norm[i]), w1, w2, tm=ROW_TILE,
                      final_g=row(final_norm) if last else None)
        else:
            w_qkv = attn_w_qkv[j]
            w_qv = jnp.concatenate([w_qkv[:, :d], w_qkv[:, 2 * d:]], axis=1).astype(BF16)
            w_kt = w_qkv[:, d:2 * d].T.astype(BF16)
            w_o = attn_w_out[j].astype(BF16)
            g = row(attn_norm[j])
            qv_m, kt_m = _qkv(hm, g, w_qv, w_kt, tm=META_ROWS, seq=META_ROWS)
            qv_r, kt_r = _qkv(hr, g, w_qv, w_kt, tm=ROW_TILE, seq=s)
            o_r = _attention(qv_r.reshape(b, s, 2 * d), kt_r, qv_m, kt_m[0], t=ATTN_TILE)
            if not last:
                o_m = _attention_small(qv_m, kt_m[0])
                hm = _mlp(hm, row(mlp_norm[i]), w1, w2, tm=META_ROWS, attn=o_m, w_o=w_o)
            hr = _mlp(hr, row(mlp_norm[i]), w1, w2, tm=ROW_TILE,
                      attn=o_r.reshape(b * s, d), w_o=w_o,
                      final_g=row(final_norm) if last else None)
    return hr.reshape(b, s, d)
```

```python
import functools
import math

import numpy as np
import jax
import jax.numpy as jnp
from jax import lax
from jax.experimental import pallas as pl
from jax.experimental.pallas import tpu as pltpu

D_MODEL = 1024
N_META = 16
N_HEADS = 16
HEAD_DIM = D_MODEL // N_HEADS
CONV_W = 3
D_FF = 4 * D_MODEL
RMS_EPS = 1e-6

LANES = 128
SUBLANES = 8
HEADS_PER_GROUP = LANES // HEAD_DIM
N_GROUPS = N_HEADS // HEADS_PER_GROUP
META_ROWS = LANES
ROW_TILE = 512
ATTN_TILE = 256
FF_CHUNK = 1024
VMEM_LIMIT_BYTES = 56 * 1024 * 1024
LOG2_E = math.log2(math.e)
MASK_BIAS = -1e30

BF16 = jnp.bfloat16
F32 = jnp.float32


def _dot(a, b):
    return jnp.dot(a, b, preferred_element_type=F32)


def _rmsnorm(x, g):
    return x * lax.rsqrt(jnp.mean(x * x, axis=-1, keepdims=True) + RMS_EPS) * g


def _resident(shape):
    zeros = (0,) * len(shape)
    return pl.BlockSpec(shape, lambda *_: zeros, pipeline_mode=pl.Buffered(1))


def _params(*semantics, flags=None):
    return pltpu.CompilerParams(dimension_semantics=semantics,
                                vmem_limit_bytes=VMEM_LIMIT_BYTES, flags=flags)


def _conv_mixer_kernel(x_ref, g_ref, w_in_ref, cw_ref, w_out_ref, hist_ref,
                       *rest, tm, tiles_per_seq, hist_row):
    if hist_row is None:
        o_ref, ubuf = rest
    else:
        o_ref, hist_out_ref, ubuf = rest
    i = pl.program_id(0)
    first = (i % tiles_per_seq) == 0

    @pl.when(first)
    def _():
        ubuf[0:SUBLANES, :] = hist_ref[...]

    @pl.when(jnp.logical_not(first))
    def _():
        ubuf[0:SUBLANES, :] = ubuf[tm:tm + SUBLANES, :]

    x = x_ref[...]
    h = _rmsnorm(x, g_ref[...]).astype(BF16)
    d = D_MODEL
    gate_c = _dot(h, w_in_ref[:, d:2 * d])
    val = _dot(h, w_in_ref[:, 2 * d:3 * d])
    ubuf[SUBLANES:SUBLANES + tm, :] = gate_c * val
    if hist_row is not None:
        hist_out_ref[...] = ubuf[hist_row:hist_row + SUBLANES, :]
    cw = cw_ref[...]
    conv = ubuf[SUBLANES:SUBLANES + tm, :] * cw[CONV_W - 1:CONV_W, :]
    for k in range(CONV_W - 1):
        back = CONV_W - 1 - k
        conv = conv + ubuf[SUBLANES - back:SUBLANES - back + tm, :] * cw[k:k + 1, :]
    gate_b = _dot(h, w_in_ref[:, 0:d])
    y = (gate_b * conv).astype(BF16)
    o_ref[...] = x + _dot(y, w_out_ref[...])


def _conv_mixer(x, g, w_in, cw, w_out, hist, *, tm, tiles_per_seq, hist_row=None):
    n, d = x.shape
    kernel = functools.partial(_conv_mixer_kernel, tm=tm,
                               tiles_per_seq=tiles_per_seq, hist_row=hist_row)
    row = pl.BlockSpec((tm, d), lambda i: (i, 0))
    out_shape = jax.ShapeDtypeStruct((n, d), F32)
    out_specs = row
    if hist_row is not None:
        out_shape = (out_shape, jax.ShapeDtypeStruct((SUBLANES, d), F32))
        out_specs = (row, pl.BlockSpec((SUBLANES, d), lambda i: (0, 0)))
    return pl.pallas_call(
        kernel,
        grid=(n // tm,),
        in_specs=[row, _resident((1, d)), _resident((d, 3 * d)),
                  _resident((CONV_W, d)), _resident((d, d)),
                  _resident((SUBLANES, d))],
        out_specs=out_specs,
        out_shape=out_shape,
        scratch_shapes=[pltpu.VMEM((tm + 2 * SUBLANES, d), F32)],
        compiler_params=_params("arbitrary"),
        name="conv_mixer",
    )(x, g, w_in, cw, w_out, hist)


def _mlp_kernel(*refs, with_proj, with_final_norm):
    refs = list(refs)
    x_ref = refs.pop(0)
    if with_proj:
        a_ref, w_o_ref = refs.pop(0), refs.pop(0)
    g_ref, w1_ref, w2_ref = refs.pop(0), refs.pop(0), refs.pop(0)
    if with_final_norm:
        gf_ref = refs.pop(0)
    (o_ref,) = refs

    x = x_ref[...]
    if with_proj:
        x = x + _dot(a_ref[...], w_o_ref[...])
    h = _rmsnorm(x, g_ref[...]).astype(BF16)
    acc = x
    for c in range(D_FF // FF_CHUNK):
        cols = slice(c * FF_CHUNK, (c + 1) * FF_CHUNK)
        a = jnp.maximum(_dot(h, w1_ref[:, cols]), 0.0)
        acc = acc + _dot((a * a).astype(BF16), w2_ref[cols, :])
    if with_final_norm:
        acc = _rmsnorm(acc, gf_ref[...])
    o_ref[...] = acc


def _mlp(x, g, w1, w2, *, tm, attn=None, w_o=None, final_g=None):
    n, d = x.shape
    with_proj = attn is not None
    with_final_norm = final_g is not None
    row = pl.BlockSpec((tm, d), lambda i: (i, 0))
    args, specs = [x], [row]
    if with_proj:
        args += [attn, w_o]
        specs += [row, _resident((d, d))]
    args += [g, w1, w2]
    specs += [_resident((1, d)), _resident((d, D_FF)), _resident((D_FF, d))]
    if with_final_norm:
        args.append(final_g)
        specs.append(_resident((1, d)))
    kernel = functools.partial(_mlp_kernel, with_proj=with_proj,
                               with_final_norm=with_final_norm)
    return pl.pallas_call(
        kernel,
        grid=(n // tm,),
        in_specs=specs,
        out_specs=row,
        out_shape=jax.ShapeDtypeStruct((n, d), F32),
        compiler_params=_params("arbitrary"),
        name="mlp",
    )(*args)


def _qkv_kernel(x_ref, g_ref, w_qv_ref, w_kt_ref, qv_ref, kt_ref):
    d = D_MODEL
    h = _rmsnorm(x_ref[...], g_ref[...]).astype(BF16)
    scale = 1.0 / math.sqrt(HEAD_DIM)
    qv_ref[:, 0:d] = (_dot(h, w_qv_ref[:, 0:d]) * scale).astype(BF16)
    qv_ref[:, d:2 * d] = _dot(h, w_qv_ref[:, d:2 * d]).astype(BF16)
    kt = lax.dot_general(w_kt_ref[...], h, (((1,), (1,)), ((), ())),
                         preferred_element_type=F32)
    kt_ref[...] = kt.astype(BF16)


def _qkv(x, g, w_qv, w_kt, *, tm, seq):
    n, d = x.shape
    tiles_per_seq = seq // tm
    return pl.pallas_call(
        _qkv_kernel,
        grid=(n // tm,),
        in_specs=[pl.BlockSpec((tm, d), lambda i: (i, 0)), _resident((1, d)),
                  _resident((d, 2 * d)), _resident((d, d))],
        out_specs=(pl.BlockSpec((tm, 2 * d), lambda i: (i, 0)),
                   pl.BlockSpec((None, d, tm),
                                lambda i: (i // tiles_per_seq, 0, i % tiles_per_seq))),
        out_shape=(jax.ShapeDtypeStruct((n, 2 * d), BF16),
                   jax.ShapeDtypeStruct((n // seq, d, seq), BF16)),
        compiler_params=_params("arbitrary"),
        name="qkv_proj",
    )(x, g, w_qv, w_kt)


def _suffix_sum_matrix(t, sign):
    r = jnp.arange(t)[:, None]
    c = jnp.arange(t + LANES)[None, :]
    return jnp.where((c >= t) | (r > c), sign, 0.0).astype(BF16)


def _head_of_lane(shape):
    return lax.broadcasted_iota(jnp.int32, shape, len(shape) - 1) // HEAD_DIM


def _attn_small_kernel(q_ref, kt_ref, v_ref, u_ref, o_ref, *, t):
    q = q_ref[...]
    head_of_lane = _head_of_lane((t, LANES))
    row_i = lax.broadcasted_iota(jnp.int32, (t, t), 0)
    col_i = lax.broadcasted_iota(jnp.int32, (t, t), 1)
    causal = col_i < row_i
    out = None
    for hh in range(HEADS_PER_GROUP):
        qh = jnp.where(head_of_lane == hh, q, jnp.zeros_like(q))
        z = _dot(qh, kt_ref[...])
        soft = jnp.log(1.0 + jnp.exp(-jnp.abs(z)))
        log_beta = jnp.minimum(z, 0.0) - soft
        log_1m = jnp.where(causal, log_beta - z, 0.0)
        tail = _dot(log_1m.astype(BF16), u_ref[:, 0:t])
        w = jnp.where(causal, jnp.exp(log_beta + tail), 0.0)
        oh = _dot(w.astype(BF16), v_ref[...])
        out = oh if out is None else jnp.where(head_of_lane == hh, oh, out)
    o_ref[...] = out.astype(BF16)


def _attention_small(qv, kt):
    t = qv.shape[0]
    d = D_MODEL
    u = _suffix_sum_matrix(t, 1.0)
    v_col0 = d // LANES
    return pl.pallas_call(
        functools.partial(_attn_small_kernel, t=t),
        grid=(N_GROUPS,),
        in_specs=[pl.BlockSpec((t, LANES), lambda g: (0, g)),
                  pl.BlockSpec((LANES, t), lambda g: (g, 0)),
                  pl.BlockSpec((t, LANES), lambda g: (0, v_col0 + g)),
                  _resident(u.shape)],
        out_specs=pl.BlockSpec((t, LANES), lambda g: (0, g)),
        out_shape=jax.ShapeDtypeStruct((t, d), BF16),
        compiler_params=_params("arbitrary"),
        name="attention_meta",
    )(qv, kt, qv, u)


_KIND_PLAIN, _KIND_DIAG, _KIND_META = 0, 1, 2
_PIPE_DEPTH = 4
_UNROLL = 12
_ITEM_FIELDS = 6
_PIPELINED_BLOCKS = 2
STEP_GROUPS = 2
LOG_UNDERFLOW = -105.0


def _tile_blocks(qi, s, t):
    blocks = [(qi * t, _KIND_DIAG)]
    blocks += [(j * t, _KIND_PLAIN) for j in range(qi - 1, -1, -1)]
    blocks.append((s, _KIND_META))
    return blocks


def _attention_items(s, t):
    n_tiles = s // t
    idle = [s, 0, _KIND_PLAIN, 1, STEP_GROUPS * n_tiles, 0]
    items = []
    for grp in range(STEP_GROUPS):
        for qi in range(n_tiles):
            for n, (k_off, kind) in enumerate(_tile_blocks(qi, s, t)[:_PIPELINED_BLOCKS]):
                items.append([qi * t, k_off, kind, int(n == 0), grp * n_tiles + qi, grp])
    lead = _PIPE_DEPTH - 1
    n_iter = len(items) + lead
    n_iter += -n_iter % _UNROLL
    table = [idle] * lead + items
    table += [idle] * (n_iter + lead - len(table))
    return np.asarray(table, np.int32).reshape(-1), n_iter


def _mask_bias(t):
    r = np.arange(t)[:, None]
    c = np.arange(t)[None, :]
    bias = np.zeros((3, t, t), np.float32)
    bias[_KIND_DIAG] = np.where(c < r, 0.0, MASK_BIAS)
    bias[_KIND_META] = np.where(c < N_META, 0.0, MASK_BIAS) + 0.0 * r
    return jnp.asarray(bias)


def _attn_pipe_kernel(tab_ref, q_ref, kt_ref, v_ref, ktm_ref, vm_ref, u_ref, bias_ref,
                      o_ref, q_all, kt_all, v_all, zbuf, lbuf, pbuf, wbuf, carry_ref,
                      acc_ref, carry_all, acc_all, obuf, *, s, t, n_iter):
    nh = HEADS_PER_GROUP
    first_step = jnp.logical_and(pl.program_id(0) == 0, pl.program_id(1) == 0)

    @pl.when(first_step)
    def _():
        for ref in (zbuf, lbuf, pbuf, wbuf, carry_ref, acc_ref):
            ref[...] = jnp.zeros_like(ref)
        q_all[:, s:s + t, :] = jnp.zeros((STEP_GROUPS * nh, t, LANES), BF16)
        v_all[:, s + META_ROWS:s + t, :] = jnp.zeros(
            (STEP_GROUPS * nh, t - META_ROWS, LANES), BF16)
        kt_all[:, :, s + META_ROWS:s + t] = jnp.zeros(
            (STEP_GROUPS, LANES, t - META_ROWS), BF16)

    head_of_lane = _head_of_lane((1, LANES))
    for grp in range(STEP_GROUPS):
        cols = slice(grp * LANES, (grp + 1) * LANES)
        q = q_ref[:, cols]
        v = v_ref[:, cols]
        vm = vm_ref[:, cols]
        for hh in range(nh):
            mine = head_of_lane == hh
            q_all[grp * nh + hh, 0:s, :] = jnp.where(mine, q, jnp.zeros_like(q))
            v_all[grp * nh + hh, 0:s, :] = jnp.where(mine, v, jnp.zeros_like(v))
            v_all[grp * nh + hh, s:s + META_ROWS, :] = jnp.where(mine, vm, jnp.zeros_like(vm))
        kt_all[grp, :, 0:s] = kt_ref[cols, :]
        kt_all[grp, :, s:s + META_ROWS] = ktm_ref[cols, :]

    def stage_scores(q_off, k_off, grp, p):
        kt_blk = kt_all[grp, :, pl.ds(pl.multiple_of(k_off, t), t)]
        for hh in range(nh):
            q_blk = q_all[grp * nh + hh, pl.ds(pl.multiple_of(q_off, t), t), :]
            zbuf[p, hh] = _dot(q_blk, kt_blk)

    def stage_logs(kind, p):
        bias = bias_ref[kind]
        for hh in range(nh):
            z = zbuf[1 - p, hh] + bias
            soft = jnp.log(1.0 + jnp.exp2(jnp.abs(z) * (-LOG2_E)))
            neg_log_1m = jnp.maximum(z, 0.0) + soft
            lbuf[p, hh] = z - neg_log_1m
            pbuf[p, hh] = neg_log_1m.astype(BF16)

    def stage_weights(first, p):
        for hh in range(nh):
            sums = _dot(pbuf[1 - p, hh], u_ref[...])
            carry = carry_ref[hh] if first is None else jnp.where(first, 0.0, carry_ref[hh])
            tail = sums[:, 0:t] + jnp.concatenate([carry] * (t // LANES), axis=1)
            wbuf[p, hh] = jnp.exp(lbuf[1 - p, hh] + tail).astype(BF16)
            carry_ref[hh] = carry + sums[:, t:t + LANES]

    def stage_output(q_off, k_off, grp, first, p):
        k_off = pl.multiple_of(k_off, t)
        w_cat = jnp.concatenate([wbuf[1 - p, hh] for hh in range(nh)], axis=1)
        v_cat = jnp.concatenate([v_all[grp * nh + hh, pl.ds(k_off, t), :]
                                 for hh in range(nh)], axis=0)
        acc = acc_ref[...] if first is None else jnp.where(first, 0.0, acc_ref[...])
        acc = acc + _dot(w_cat, v_cat)
        acc_ref[...] = acc
        obuf[grp, pl.ds(pl.multiple_of(q_off, t), t), :] = acc.astype(BF16)

    def field(item, f):
        return tab_ref[item * _ITEM_FIELDS + f]

    def iteration(i, p):
        stage_logs(field(i + 2, 2), p)
        stage_weights(field(i + 1, 3) != 0, p)
        carry_all[field(i + 1, 4)] = carry_ref[...]
        stage_scores(field(i + 3, 0), field(i + 3, 1), field(i + 3, 5), p)
        stage_output(field(i, 0), field(i, 1), field(i, 5), field(i, 3) != 0, p)
        acc_all[field(i, 4)] = acc_ref[...]

    def body(k, c):
        for j in range(_UNROLL):
            iteration(_UNROLL * k + j, j % 2)
        return c

    lax.fori_loop(0, n_iter // _UNROLL, body, 0)

    def saturated():
        return jnp.max(carry_ref[...]) <= LOG_UNDERFLOW

    n_tiles = s // t
    first_open = _PIPELINED_BLOCKS - 1

    def finish_tile(slot, c):
        grp = slot // n_tiles
        qi = slot % n_tiles
        n_blocks = qi + 2
        carry_ref[...] = carry_all[slot]

        @pl.when(jnp.logical_and(qi >= first_open, jnp.logical_not(saturated())))
        def _():
            acc_ref[...] = acc_all[slot]
            q_off = qi * t

            def more(state):
                n, done = state
                return jnp.logical_and(n < n_blocks, jnp.logical_not(done))

            def one_block(state):
                n, _ = state
                is_meta = n == n_blocks - 1
                k_off = jnp.where(is_meta, s, (qi - n) * t)
                kind = jnp.where(is_meta, _KIND_META, _KIND_PLAIN)
                stage_scores(q_off, k_off, grp, 0)
                stage_logs(kind, 1)
                stage_weights(None, 0)
                stage_output(q_off, k_off, grp, None, 1)
                return n + 1, saturated()

            lax.while_loop(more, one_block, (jnp.int32(_PIPELINED_BLOCKS), jnp.bool_(False)))

        return c

    open_slots = [grp * n_tiles + qi for grp in range(STEP_GROUPS)
                  for qi in range(first_open, n_tiles)]
    worst = carry_all[open_slots[0]]
    for slot in open_slots[1:]:
        worst = jnp.maximum(worst, carry_all[slot])

    @pl.when(jnp.max(worst) > LOG_UNDERFLOW)
    def _():
        lax.fori_loop(0, STEP_GROUPS * n_tiles, finish_tile, 0)

    for grp in range(STEP_GROUPS):
        o_ref[:, grp * LANES:(grp + 1) * LANES] = obuf[grp, 0:s, :]


def _attention(qv, kt, meta_qv, meta_kt, *, t):
    b, s, _ = qv.shape
    d = D_MODEL
    nh = HEADS_PER_GROUP
    table, n_iter = _attention_items(s, t)
    u = _suffix_sum_matrix(t, -1.0)
    bias = _mask_bias(t)
    width = STEP_GROUPS * LANES
    v_col0 = d // width
    n_slots = STEP_GROUPS * (s // t) + 1
    grid_spec = pltpu.PrefetchScalarGridSpec(
        num_scalar_prefetch=1,
        grid=(b, N_GROUPS // STEP_GROUPS),
        in_specs=[
            pl.BlockSpec((None, s, width), lambda bi, g, tab: (bi, 0, g)),
            pl.BlockSpec((None, width, s), lambda bi, g, tab: (bi, g, 0)),
            pl.BlockSpec((None, s, width), lambda bi, g, tab: (bi, 0, v_col0 + g)),
            pl.BlockSpec((width, META_ROWS), lambda bi, g, tab: (g, 0)),
            pl.BlockSpec((META_ROWS, width), lambda bi, g, tab: (0, v_col0 + g)),
            pl.BlockSpec(u.shape, lambda bi, g, tab: (0, 0)),
            pl.BlockSpec(bias.shape, lambda bi, g, tab: (0, 0, 0)),
        ],
        out_specs=pl.BlockSpec((None, s, width), lambda bi, g, tab: (bi, 0, g)),
        scratch_shapes=[
            pltpu.VMEM((STEP_GROUPS * nh, s + t, LANES), BF16),
            pltpu.VMEM((STEP_GROUPS, LANES, s + t), BF16),
            pltpu.VMEM((STEP_GROUPS * nh, s + t, LANES), BF16),
            pltpu.VMEM((2, nh, t, t), F32),
            pltpu.VMEM((2, nh, t, t), F32),
            pltpu.VMEM((2, nh, t, t), BF16),
            pltpu.VMEM((2, nh, t, t), BF16),
            pltpu.VMEM((nh, t, LANES), F32),
            pltpu.VMEM((t, LANES), F32),
            pltpu.VMEM((n_slots, nh, t, LANES), F32),
            pltpu.VMEM((n_slots, t, LANES), F32),
            pltpu.VMEM((STEP_GROUPS, s + t, LANES), BF16),
        ],
    )
    kernel = functools.partial(_attn_pipe_kernel, s=s, t=t, n_iter=n_iter)
    return pl.pallas_call(
        kernel,
        grid_spec=grid_spec,
        out_shape=jax.ShapeDtypeStruct((b, s, d), BF16),
        compiler_params=_params("arbitrary", "arbitrary"),
        name="stick_breaking_attention",
    )(jnp.asarray(table), qv, kt, qv, meta_kt, meta_qv, u, bias)


def kernel(x, meta_tokens, conv_norm, conv_w_in, conv_w, conv_w_out, attn_norm,
           attn_w_qkv, attn_w_out, mlp_norm, mlp_w1, mlp_w2, final_norm):
    b, s, d = x.shape
    depth = mlp_norm.shape[0]
    hr = x.reshape(b * s, d)
    hm = jnp.pad(meta_tokens.astype(x.dtype), ((0, META_ROWS - N_META), (0, 0)))
    zero_hist = jnp.zeros((SUBLANES, d), F32)
    row = lambda v: v.reshape(1, d)

    for i in range(depth):
        j = i // 2
        last = i == depth - 1
        w1 = mlp_w1[i].astype(BF16)
        w2 = mlp_w2[i].astype(BF16)
        if i % 2 == 0:
            w_in = conv_w_in[j].astype(BF16)
            w_out = conv_w_out[j].astype(BF16)
            g = row(conv_norm[j])
            hm, hist = _conv_mixer(hm, g, w_in, conv_w[j], w_out, zero_hist,
                                   tm=META_ROWS, tiles_per_seq=1, hist_row=N_META)
            hr = _conv_mixer(hr, g, w_in, conv_w[j], w_out, hist,
                             tm=ROW_TILE, tiles_per_seq=s // ROW_TILE)
            hm = _mlp(hm, row(mlp_norm[i]), w1, w2, tm=META_ROWS)
            hr = _mlp(hr, row(mlp_norm[i]), w1, w2, tm=ROW_TILE,
                      final_g=row(final_norm) if last else None)
        else:
            w_qkv = attn_w_qkv[j]
            w_qv = jnp.concatenate([w_qkv[:, :d], w_qkv[:, 2 * d:]], axis=1).astype(BF16)
            w_kt = w_qkv[:, d:2 * d].T.astype(BF16)
            w_o = attn_w_out[j].astype(BF16)
            g = row(attn_norm[j])
            qv_m, kt_m = _qkv(hm, g, w_qv, w_kt, tm=META_ROWS, seq=META_ROWS)
            qv_r, kt_r = _qkv(hr, g, w_qv, w_kt, tm=ROW_TILE, seq=s)
            o_r = _attention(qv_r.reshape(b, s, 2 * d), kt_r, qv_m, kt_m[0], t=ATTN_TILE)
            if not last:
                o_m = _attention_small(qv_m, kt_m[0])
                hm = _mlp(hm, row(mlp_norm[i]), w1, w2, tm=META_ROWS, attn=o_m, w_o=w_o)
            hr = _mlp(hr, row(mlp_norm[i]), w1, w2, tm=ROW_TILE,
                      attn=o_r.reshape(b * s, d), w_o=w_o,
                      final_g=row(final_norm) if last else None)
    return hr.reshape(b, s, d)
```

```python
import functools
import math

import numpy as np
import jax
import jax.numpy as jnp
from jax import lax
from jax.experimental import pallas as pl
from jax.experimental.pallas import tpu as pltpu

D_MODEL = 1024
N_META = 16
N_HEADS = 16
HEAD_DIM = D_MODEL // N_HEADS
CONV_W = 3
D_FF = 4 * D_MODEL
RMS_EPS = 1e-6

LANES = 128
SUBLANES = 8
HEADS_PER_GROUP = LANES // HEAD_DIM
N_GROUPS = N_HEADS // HEADS_PER_GROUP
META_ROWS = LANES
ROW_TILE = 512
ATTN_TILE = 256
FF_CHUNK = 1024
VMEM_LIMIT_BYTES = 56 * 1024 * 1024
LOG2_E = math.log2(math.e)
MASK_BIAS = -1e30

BF16 = jnp.bfloat16
F32 = jnp.float32


def _dot(a, b):
    return jnp.dot(a, b, preferred_element_type=F32)


def _rmsnorm(x, g):
    return x * lax.rsqrt(jnp.mean(x * x, axis=-1, keepdims=True) + RMS_EPS) * g


def _resident(shape, layer=None):
    zeros = (0,) * len(shape)
    if layer is None:
        return pl.BlockSpec(shape, lambda *_: zeros, pipeline_mode=pl.Buffered(1))
    return pl.BlockSpec((None,) + tuple(shape), lambda *_: (layer,) + zeros,
                        pipeline_mode=pl.Buffered(1))


def _params(*semantics, flags=None):
    return pltpu.CompilerParams(dimension_semantics=semantics,
                                vmem_limit_bytes=VMEM_LIMIT_BYTES, flags=flags)


def _conv_mixer_kernel(x_ref, g_ref, w_in_ref, cw_ref, w_out_ref, hist_ref,
                       *rest, tm, tiles_per_seq, hist_row):
    if hist_row is None:
        o_ref, ubuf = rest
    else:
        o_ref, hist_out_ref, ubuf = rest
    i = pl.program_id(0)
    first = (i % tiles_per_seq) == 0

    @pl.when(first)
    def _():
        ubuf[0:SUBLANES, :] = hist_ref[...]

    @pl.when(jnp.logical_not(first))
    def _():
        ubuf[0:SUBLANES, :] = ubuf[tm:tm + SUBLANES, :]

    x = x_ref[...]
    h = _rmsnorm(x, g_ref[...]).astype(BF16)
    d = D_MODEL
    gate_c = _dot(h, w_in_ref[:, d:2 * d])
    val = _dot(h, w_in_ref[:, 2 * d:3 * d])
    ubuf[SUBLANES:SUBLANES + tm, :] = gate_c * val
    if hist_row is not None:
        hist_out_ref[...] = ubuf[hist_row:hist_row + SUBLANES, :]
    cw = cw_ref[...]
    conv = ubuf[SUBLANES:SUBLANES + tm, :] * cw[CONV_W - 1:CONV_W, :]
    for k in range(CONV_W - 1):
        back = CONV_W - 1 - k
        conv = conv + ubuf[SUBLANES - back:SUBLANES - back + tm, :] * cw[k:k + 1, :]
    gate_b = _dot(h, w_in_ref[:, 0:d])
    y = (gate_b * conv).astype(BF16)
    o_ref[...] = x + _dot(y, w_out_ref[...])


def _conv_mixer(x, g, w_in, cw, w_out, hist, *, layer, tm, tiles_per_seq, hist_row=None):
    n, d = x.shape
    kernel = functools.partial(_conv_mixer_kernel, tm=tm,
                               tiles_per_seq=tiles_per_seq, hist_row=hist_row)
    row = pl.BlockSpec((tm, d), lambda i: (i, 0))
    out_shape = jax.ShapeDtypeStruct((n, d), F32)
    out_specs = row
    if hist_row is not None:
        out_shape = (out_shape, jax.ShapeDtypeStruct((SUBLANES, d), F32))
        out_specs = (row, pl.BlockSpec((SUBLANES, d), lambda i: (0, 0)))
    return pl.pallas_call(
        kernel,
        grid=(n // tm,),
        in_specs=[row, _resident((1, d), layer), _resident((d, 3 * d), layer),
                  _resident((CONV_W, d), layer), _resident((d, d), layer),
                  _resident((SUBLANES, d))],
        out_specs=out_specs,
        out_shape=out_shape,
        scratch_shapes=[pltpu.VMEM((tm + 2 * SUBLANES, d), F32)],
        compiler_params=_params("arbitrary"),
        name="conv_mixer",
    )(x, g, w_in, cw, w_out, hist)


def _mlp_kernel(*refs, with_proj, with_final_norm):
    refs = list(refs)
    x_ref = refs.pop(0)
    if with_proj:
        a_ref, w_o_ref = refs.pop(0), refs.pop(0)
    g_ref, w1_ref, w2_ref = refs.pop(0), refs.pop(0), refs.pop(0)
    if with_final_norm:
        gf_ref = refs.pop(0)
    (o_ref,) = refs

    x = x_ref[...]
    if with_proj:
        x = x + _dot(a_ref[...], w_o_ref[...])
    h = _rmsnorm(x, g_ref[...]).astype(BF16)
    acc = x
    for c in range(D_FF // FF_CHUNK):
        cols = slice(c * FF_CHUNK, (c + 1) * FF_CHUNK)
        a = jnp.maximum(_dot(h, w1_ref[:, cols]), 0.0)
        acc = acc + _dot((a * a).astype(BF16), w2_ref[cols, :])
    if with_final_norm:
        acc = _rmsnorm(acc, gf_ref[...])
    o_ref[...] = acc


def _mlp(x, g, w1, w2, *, layer, tm, attn=None, w_o=None, w_o_layer=None, final_g=None):
    n, d = x.shape
    with_proj = attn is not None
    with_final_norm = final_g is not None
    row = pl.BlockSpec((tm, d), lambda i: (i, 0))
    args, specs = [x], [row]
    if with_proj:
        args += [attn, w_o]
        specs += [row, _resident((d, d), w_o_layer)]
    args += [g, w1, w2]
    specs += [_resident((1, d), layer), _resident((d, D_FF), layer),
              _resident((D_FF, d), layer)]
    if with_final_norm:
        args.append(final_g)
        specs.append(_resident((1, d)))
    kernel = functools.partial(_mlp_kernel, with_proj=with_proj,
                               with_final_norm=with_final_norm)
    return pl.pallas_call(
        kernel,
        grid=(n // tm,),
        in_specs=specs,
        out_specs=row,
        out_shape=jax.ShapeDtypeStruct((n, d), F32),
        compiler_params=_params("arbitrary"),
        name="mlp",
    )(*args)


def _qkv_kernel(x_ref, g_ref, w_qkv_ref, w_kt_ref, qv_ref, kt_ref):
    d = D_MODEL
    h = _rmsnorm(x_ref[...], g_ref[...]).astype(BF16)
    scale = 1.0 / math.sqrt(HEAD_DIM)
    qv_ref[:, 0:d] = (_dot(h, w_qkv_ref[:, 0:d]) * scale).astype(BF16)
    qv_ref[:, d:2 * d] = _dot(h, w_qkv_ref[:, 2 * d:3 * d]).astype(BF16)
    kt = lax.dot_general(w_kt_ref[...], h, (((1,), (1,)), ((), ())),
                         preferred_element_type=F32)
    kt_ref[...] = kt.astype(BF16)


def _qkv(x, g, w_qkv, w_kt, *, layer, tm, seq):
    n, d = x.shape
    tiles_per_seq = seq // tm
    return pl.pallas_call(
        _qkv_kernel,
        grid=(n // tm,),
        in_specs=[pl.BlockSpec((tm, d), lambda i: (i, 0)), _resident((1, d), layer),
                  _resident((d, 3 * d), layer), _resident((d, d), layer)],
        out_specs=(pl.BlockSpec((tm, 2 * d), lambda i: (i, 0)),
                   pl.BlockSpec((None, d, tm),
                                lambda i: (i // tiles_per_seq, 0, i % tiles_per_seq))),
        out_shape=(jax.ShapeDtypeStruct((n, 2 * d), BF16),
                   jax.ShapeDtypeStruct((n // seq, d, seq), BF16)),
        compiler_params=_params("arbitrary"),
        name="qkv_proj",
    )(x, g, w_qkv, w_kt)


def _suffix_sum_matrix(t, sign):
    r = jnp.arange(t)[:, None]
    c = jnp.arange(t + LANES)[None, :]
    return jnp.where((c >= t) | (r > c), sign, 0.0).astype(BF16)


def _head_of_lane(shape):
    return lax.broadcasted_iota(jnp.int32, shape, len(shape) - 1) // HEAD_DIM


def _attn_small_kernel(q_ref, kt_ref, v_ref, u_ref, o_ref, *, t):
    q = q_ref[...]
    head_of_lane = _head_of_lane((t, LANES))
    row_i = lax.broadcasted_iota(jnp.int32, (t, t), 0)
    col_i = lax.broadcasted_iota(jnp.int32, (t, t), 1)
    causal = col_i < row_i
    out = None
    for hh in range(HEADS_PER_GROUP):
        qh = jnp.where(head_of_lane == hh, q, jnp.zeros_like(q))
        z = _dot(qh, kt_ref[...])
        soft = jnp.log(1.0 + jnp.exp(-jnp.abs(z)))
        log_beta = jnp.minimum(z, 0.0) - soft
        log_1m = jnp.where(causal, log_beta - z, 0.0)
        tail = _dot(log_1m.astype(BF16), u_ref[:, 0:t])
        w = jnp.where(causal, jnp.exp(log_beta + tail), 0.0)
        oh = _dot(w.astype(BF16), v_ref[...])
        out = oh if out is None else jnp.where(head_of_lane == hh, oh, out)
    o_ref[...] = out.astype(BF16)


def _attention_small(qv, kt):
    t = qv.shape[0]
    d = D_MODEL
    u = _suffix_sum_matrix(t, 1.0)
    v_col0 = d // LANES
    return pl.pallas_call(
        functools.partial(_attn_small_kernel, t=t),
        grid=(N_GROUPS,),
        in_specs=[pl.BlockSpec((t, LANES), lambda g: (0, g)),
                  pl.BlockSpec((LANES, t), lambda g: (g, 0)),
                  pl.BlockSpec((t, LANES), lambda g: (0, v_col0 + g)),
                  _resident(u.shape)],
        out_specs=pl.BlockSpec((t, LANES), lambda g: (0, g)),
        out_shape=jax.ShapeDtypeStruct((t, d), BF16),
        compiler_params=_params("arbitrary"),
        name="attention_meta",
    )(qv, kt, qv, u)


_KIND_PLAIN, _KIND_DIAG, _KIND_META = 0, 1, 2
_PIPE_DEPTH = 4
_UNROLL = 12
_ITEM_FIELDS = 7
_PIPELINED_BLOCKS = 2
STEP_GROUPS = 2
LOG_UNDERFLOW = -105.0


def _tile_blocks(qi, s, t):
    blocks = [(qi * t, _KIND_DIAG)]
    blocks += [(j * t, _KIND_PLAIN) for j in range(qi - 1, -1, -1)]
    blocks.append((s, _KIND_META))
    return blocks


def _attention_items(s, t):
    n_tiles = s // t
    idle = [s, 0, _KIND_PLAIN, 1, STEP_GROUPS * n_tiles, 0, 0]
    items = []
    for grp in range(STEP_GROUPS):
        for qi in range(n_tiles):
            for n, (k_off, kind) in enumerate(_tile_blocks(qi, s, t)[:_PIPELINED_BLOCKS]):
                items.append([qi * t, k_off, kind, int(n == 0), grp * n_tiles + qi, grp,
                              qi * t])
    lead = _PIPE_DEPTH - 1
    n_iter = len(items) + lead
    n_iter += -n_iter % _UNROLL
    table = [idle] * lead + items
    table += [idle] * (n_iter + lead - len(table))
    return np.asarray(table, np.int32).reshape(-1), n_iter


def _mask_bias(t):
    r = np.arange(t)[:, None]
    c = np.arange(t)[None, :]
    bias = np.zeros((3, t, t), np.float32)
    bias[_KIND_DIAG] = np.where(c < r, 0.0, MASK_BIAS)
    bias[_KIND_META] = np.where(c < N_META, 0.0, MASK_BIAS) + 0.0 * r
    return jnp.asarray(bias)


def _attn_pipe_kernel(tab_ref, q_ref, kt_ref, v_ref, ktm_ref, vm_ref, u_ref, bias_ref,
                      o_ref, kt_all, v_all, zbuf, lbuf, pbuf, wbuf, carry_ref,
                      acc_ref, carry_all, acc_all, obuf, open_ref, *, s, t, n_iter):
    nh = HEADS_PER_GROUP
    first_step = jnp.logical_and(pl.program_id(0) == 0, pl.program_id(1) == 0)

    @pl.when(first_step)
    def _():
        for ref in (zbuf, lbuf, pbuf, wbuf, carry_ref, acc_ref):
            ref[...] = jnp.zeros_like(ref)
        kt_all[...] = jnp.zeros_like(kt_all)
        v_all[:, s + META_ROWS:s + t, :] = jnp.zeros(
            (STEP_GROUPS * nh, t - META_ROWS, LANES), BF16)

    head_of_lane = _head_of_lane((1, LANES))
    for grp in range(STEP_GROUPS):
        cols = slice(grp * LANES, (grp + 1) * LANES)
        v = v_ref[:, cols]
        vm = vm_ref[:, cols]
        for hh in range(nh):
            mine = head_of_lane == hh
            v_all[grp * nh + hh, 0:s, :] = jnp.where(mine, v, jnp.zeros_like(v))
            v_all[grp * nh + hh, s:s + META_ROWS, :] = jnp.where(mine, vm, jnp.zeros_like(vm))
            rows = slice(hh * HEAD_DIM, (hh + 1) * HEAD_DIM)
            src = slice(grp * LANES + hh * HEAD_DIM, grp * LANES + (hh + 1) * HEAD_DIM)
            kt_all[grp * nh + hh, rows, 0:s] = kt_ref[src, :]
            kt_all[grp * nh + hh, rows, s:s + META_ROWS] = ktm_ref[src, :]

    def stage_scores(q_off, k_off, grp, p):
        q_blk = q_ref[pl.ds(pl.multiple_of(q_off, t), t),
                      pl.ds(pl.multiple_of(grp * LANES, LANES), LANES)]
        for hh in range(nh):
            zbuf[p, hh] = _dot(q_blk, kt_all[grp * nh + hh, :, pl.ds(pl.multiple_of(k_off, t), t)])

    def stage_logs(kind, p):
        bias = bias_ref[kind]
        for hh in range(nh):
            z = zbuf[1 - p, hh] + bias
            soft = jnp.log(1.0 + jnp.exp2(jnp.abs(z) * (-LOG2_E)))
            neg_log_1m = jnp.maximum(z, 0.0) + soft
            lbuf[p, hh] = z - neg_log_1m
            pbuf[p, hh] = neg_log_1m.astype(BF16)

    def stage_weights(first, p):
        for hh in range(nh):
            sums = _dot(pbuf[1 - p, hh], u_ref[...])
            carry = carry_ref[hh] if first is None else jnp.where(first, 0.0, carry_ref[hh])
            tail = sums[:, 0:t] + jnp.concatenate([carry] * (t // LANES), axis=1)
            wbuf[p, hh] = jnp.exp(lbuf[1 - p, hh] + tail).astype(BF16)
            carry_ref[hh] = carry + sums[:, t:t + LANES]

    def stage_output(q_off, k_off, grp, first, p):
        k_off = pl.multiple_of(k_off, t)
        w_cat = jnp.concatenate([wbuf[1 - p, hh] for hh in range(nh)], axis=1)
        v_cat = jnp.concatenate([v_all[grp * nh + hh, pl.ds(k_off, t), :]
                                 for hh in range(nh)], axis=0)
        acc = acc_ref[...] if first is None else jnp.where(first, 0.0, acc_ref[...])
        acc = acc + _dot(w_cat, v_cat)
        acc_ref[...] = acc
        obuf[grp, pl.ds(pl.multiple_of(q_off, t), t), :] = acc.astype(BF16)

    def field(item, f):
        return tab_ref[item * _ITEM_FIELDS + f]

    def iteration(i, p):
        stage_logs(field(i + 2, 2), p)
        stage_weights(field(i + 1, 3) != 0, p)
        carry_all[field(i + 1, 4)] = carry_ref[...]
        stage_scores(field(i + 3, 6), field(i + 3, 1), field(i + 3, 5), p)
        stage_output(field(i, 0), field(i, 1), field(i, 5), field(i, 3) != 0, p)
        acc_all[field(i, 4)] = acc_ref[...]

    def body(k, c):
        for j in range(_UNROLL):
            iteration(_UNROLL * k + j, j % 2)
        return c

    lax.fori_loop(0, n_iter // _UNROLL, body, 0)

    def saturated():
        return jnp.max(carry_ref[...]) <= LOG_UNDERFLOW

    n_tiles = s // t
    first_open = _PIPELINED_BLOCKS - 1

    def finish_tile(slot, c):
        grp = slot // n_tiles
        qi = slot % n_tiles
        n_blocks = qi + 2

        @pl.when(open_ref[slot] != 0)
        def _():
            carry_ref[...] = carry_all[slot]
            acc_ref[...] = acc_all[slot]
            q_off = qi * t

            def more(state):
                n, done = state
                return jnp.logical_and(n < n_blocks, jnp.logical_not(done))

            def one_block(state):
                n, _ = state
                is_meta = n == n_blocks - 1
                k_off = jnp.where(is_meta, s, (qi - n) * t)
                kind = jnp.where(is_meta, _KIND_META, _KIND_PLAIN)
                stage_scores(q_off, k_off, grp, 0)
                stage_logs(kind, 1)
                stage_weights(None, 0)
                stage_output(q_off, k_off, grp, None, 1)
                return n + 1, saturated()

            lax.while_loop(more, one_block, (jnp.int32(_PIPELINED_BLOCKS), jnp.bool_(False)))

        return c

    open_slots = [grp * n_tiles + qi for grp in range(STEP_GROUPS)
                  for qi in range(first_open, n_tiles)]
    worst = carry_all[open_slots[0]]
    for slot in open_slots[1:]:
        worst = jnp.maximum(worst, carry_all[slot])

    @pl.when(jnp.max(worst) > LOG_UNDERFLOW)
    def _():
        for slot in range(STEP_GROUPS * n_tiles):
            if slot in open_slots:
                open_ref[slot] = (jnp.max(carry_all[slot]) > LOG_UNDERFLOW).astype(jnp.int32)
            else:
                open_ref[slot] = 0
        lax.fori_loop(0, STEP_GROUPS * n_tiles, finish_tile, 0)

    for grp in range(STEP_GROUPS):
        o_ref[:, grp * LANES:(grp + 1) * LANES] = obuf[grp, 0:s, :]


def _attention(qv, kt, meta_qv, meta_kt, *, t):
    b, s, _ = qv.shape
    d = D_MODEL
    nh = HEADS_PER_GROUP
    table, n_iter = _attention_items(s, t)
    u = _suffix_sum_matrix(t, -1.0)
    bias = _mask_bias(t)
    width = STEP_GROUPS * LANES
    v_col0 = d // width
    n_slots = STEP_GROUPS * (s // t) + 1
    grid_spec = pltpu.PrefetchScalarGridSpec(
        num_scalar_prefetch=1,
        grid=(b, N_GROUPS // STEP_GROUPS),
        in_specs=[
            pl.BlockSpec((None, s, width), lambda bi, g, tab: (bi, 0, g)),
            pl.BlockSpec((None, width, s), lambda bi, g, tab: (bi, g, 0)),
            pl.BlockSpec((None, s, width), lambda bi, g, tab: (bi, 0, v_col0 + g)),
            pl.BlockSpec((width, META_ROWS), lambda bi, g, tab: (g, 0)),
            pl.BlockSpec((META_ROWS, width), lambda bi, g, tab: (0, v_col0 + g)),
            pl.BlockSpec(u.shape, lambda bi, g, tab: (0, 0)),
            pl.BlockSpec(bias.shape, lambda bi, g, tab: (0, 0, 0)),
        ],
        out_specs=pl.BlockSpec((None, s, width), lambda bi, g, tab: (bi, 0, g)),
        scratch_shapes=[
            pltpu.VMEM((STEP_GROUPS * nh, LANES, s + t), BF16),
            pltpu.VMEM((STEP_GROUPS * nh, s + t, LANES), BF16),
            pltpu.VMEM((2, nh, t, t), F32),
            pltpu.VMEM((2, nh, t, t), F32),
            pltpu.VMEM((2, nh, t, t), BF16),
            pltpu.VMEM((2, nh, t, t), BF16),
            pltpu.VMEM((nh, t, LANES), F32),
            pltpu.VMEM((t, LANES), F32),
            pltpu.VMEM((n_slots, nh, t, LANES), F32),
            pltpu.VMEM((n_slots, t, LANES), F32),
            pltpu.VMEM((STEP_GROUPS, s + t, LANES), BF16),
            pltpu.SMEM((n_slots,), jnp.int32),
        ],
    )
    kernel = functools.partial(_attn_pipe_kernel, s=s, t=t, n_iter=n_iter)
    return pl.pallas_call(
        kernel,
        grid_spec=grid_spec,
        out_shape=jax.ShapeDtypeStruct((b, s, d), BF16),
        compiler_params=_params("arbitrary", "arbitrary"),
        name="stick_breaking_attention",
    )(jnp.asarray(table), qv, kt, qv, meta_kt, meta_qv, u, bias)


def kernel(x, meta_tokens, conv_norm, conv_w_in, conv_w, conv_w_out, attn_norm,
           attn_w_qkv, attn_w_out, mlp_norm, mlp_w1, mlp_w2, final_norm):
    b, s, d = x.shape
    depth = mlp_norm.shape[0]
    hr = x.reshape(b * s, d)
    hm = jnp.pad(meta_tokens.astype(x.dtype), ((0, META_ROWS - N_META), (0, 0)))
    zero_hist = jnp.zeros((SUBLANES, d), F32)
    rows = lambda g: g.reshape(-1, 1, d)

    w1, w2 = mlp_w1.astype(BF16), mlp_w2.astype(BF16)
    w_in, w_out = conv_w_in.astype(BF16), conv_w_out.astype(BF16)
    w_qkv, w_o = attn_w_qkv.astype(BF16), attn_w_out.astype(BF16)
    w_kt = jnp.transpose(attn_w_qkv[:, :, d:2 * d], (0, 2, 1)).astype(BF16)
    g_conv, g_attn, g_mlp = rows(conv_norm), rows(attn_norm), rows(mlp_norm)
    g_final = final_norm.reshape(1, d)

    for i in range(depth):
        j = i // 2
        last = i == depth - 1
        final_g = g_final if last else None
        if i % 2 == 0:
            hm, hist = _conv_mixer(hm, g_conv, w_in, conv_w, w_out, zero_hist, layer=j,
                                   tm=META_ROWS, tiles_per_seq=1, hist_row=N_META)
            hr = _conv_mixer(hr, g_conv, w_in, conv_w, w_out, hist, layer=j,
                             tm=ROW_TILE, tiles_per_seq=s // ROW_TILE)
            hm = _mlp(hm, g_mlp, w1, w2, layer=i, tm=META_ROWS)
            hr = _mlp(hr, g_mlp, w1, w2, layer=i, tm=ROW_TILE, final_g=final_g)
        else:
            qv_m, kt_m = _qkv(hm, g_attn, w_qkv, w_kt, layer=j, tm=META_ROWS, seq=META_ROWS)
            qv_r, kt_r = _qkv(hr, g_attn, w_qkv, w_kt, layer=j, tm=ROW_TILE, seq=s)
            o_r = _attention(qv_r.reshape(b, s, 2 * d), kt_r, qv_m, kt_m[0], t=ATTN_TILE)
            if not last:
                o_m = _attention_small(qv_m, kt_m[0])
                hm = _mlp(hm, g_mlp, w1, w2, layer=i, tm=META_ROWS,
                          attn=o_m, w_o=w_o, w_o_layer=j)
            hr = _mlp(hr, g_mlp, w1, w2, layer=i, tm=ROW_TILE,
                      attn=o_r.reshape(b * s, d), w_o=w_o, w_o_layer=j, final_g=final_g)
    return hr.reshape(b, s, d)
```

```python
import functools
import math

import numpy as np
import jax
import jax.numpy as jnp
from jax import lax
from jax.experimental import pallas as pl
from jax.experimental.pallas import tpu as pltpu

D_MODEL = 1024
N_META = 16
N_HEADS = 16
HEAD_DIM = D_MODEL // N_HEADS
CONV_W = 3
D_FF = 4 * D_MODEL
RMS_EPS = 1e-6

LANES = 128
SUBLANES = 8
HEADS_PER_GROUP = LANES // HEAD_DIM
N_GROUPS = N_HEADS // HEADS_PER_GROUP
META_ROWS = LANES
ROW_TILE = 512
ATTN_TILE = 256
FF_CHUNK = 1024
VMEM_LIMIT_BYTES = 56 * 1024 * 1024
LOG2_E = math.log2(math.e)
MASK_BIAS = -1e30

BF16 = jnp.bfloat16
F32 = jnp.float32


def _dot(a, b):
    return jnp.dot(a, b, preferred_element_type=F32)


def _rmsnorm(x, g):
    return x * lax.rsqrt(jnp.mean(x * x, axis=-1, keepdims=True) + RMS_EPS) * g


def _resident(shape, layer=None):
    zeros = (0,) * len(shape)
    if layer is None:
        return pl.BlockSpec(shape, lambda *_: zeros, pipeline_mode=pl.Buffered(1))
    return pl.BlockSpec((None,) + tuple(shape), lambda *_: (layer,) + zeros,
                        pipeline_mode=pl.Buffered(1))


def _params(*semantics, flags=None):
    return pltpu.CompilerParams(dimension_semantics=semantics,
                                vmem_limit_bytes=VMEM_LIMIT_BYTES, flags=flags)


def _conv_mixer_kernel(x_ref, g_ref, w_in_ref, cw_ref, w_out_ref, hist_ref,
                       *rest, tm, tiles_per_seq, hist_row):
    if hist_row is None:
        o_ref, ubuf = rest
    else:
        o_ref, hist_out_ref, ubuf = rest
    i = pl.program_id(0)
    first = (i % tiles_per_seq) == 0

    @pl.when(first)
    def _():
        ubuf[0:SUBLANES, :] = hist_ref[...]

    @pl.when(jnp.logical_not(first))
    def _():
        ubuf[0:SUBLANES, :] = ubuf[tm:tm + SUBLANES, :]

    x = x_ref[...]
    h = _rmsnorm(x, g_ref[...]).astype(BF16)
    d = D_MODEL
    gate_c = _dot(h, w_in_ref[:, d:2 * d])
    val = _dot(h, w_in_ref[:, 2 * d:3 * d])
    ubuf[SUBLANES:SUBLANES + tm, :] = gate_c * val
    if hist_row is not None:
        hist_out_ref[...] = ubuf[hist_row:hist_row + SUBLANES, :]
    cw = cw_ref[...]
    conv = ubuf[SUBLANES:SUBLANES + tm, :] * cw[CONV_W - 1:CONV_W, :]
    for k in range(CONV_W - 1):
        back = CONV_W - 1 - k
        conv = conv + ubuf[SUBLANES - back:SUBLANES - back + tm, :] * cw[k:k + 1, :]
    gate_b = _dot(h, w_in_ref[:, 0:d])
    y = (gate_b * conv).astype(BF16)
    o_ref[...] = x + _dot(y, w_out_ref[...])


def _conv_mixer(x, g, w_in, cw, w_out, hist, *, layer, tm, tiles_per_seq, hist_row=None):
    n, d = x.shape
    kernel = functools.partial(_conv_mixer_kernel, tm=tm,
                               tiles_per_seq=tiles_per_seq, hist_row=hist_row)
    row = pl.BlockSpec((tm, d), lambda i: (i, 0))
    out_shape = jax.ShapeDtypeStruct((n, d), F32)
    out_specs = row
    if hist_row is not None:
        out_shape = (out_shape, jax.ShapeDtypeStruct((SUBLANES, d), F32))
        out_specs = (row, pl.BlockSpec((SUBLANES, d), lambda i: (0, 0)))
    return pl.pallas_call(
        kernel,
        grid=(n // tm,),
        in_specs=[row, _resident((1, d), layer), _resident((d, 3 * d), layer),
                  _resident((CONV_W, d), layer), _resident((d, d), layer),
                  _resident((SUBLANES, d))],
        out_specs=out_specs,
        out_shape=out_shape,
        scratch_shapes=[pltpu.VMEM((tm + 2 * SUBLANES, d), F32)],
        compiler_params=_params("arbitrary"),
        name="conv_mixer",
    )(x, g, w_in, cw, w_out, hist)


def _mlp_kernel(*refs, with_proj, with_final_norm):
    refs = list(refs)
    x_ref = refs.pop(0)
    if with_proj:
        a_ref, w_o_ref = refs.pop(0), refs.pop(0)
    g_ref, w1_ref, w2_ref = refs.pop(0), refs.pop(0), refs.pop(0)
    if with_final_norm:
        gf_ref = refs.pop(0)
    (o_ref,) = refs

    x = x_ref[...]
    if with_proj:
        x = x + _dot(a_ref[...], w_o_ref[...])
    h = _rmsnorm(x, g_ref[...]).astype(BF16)
    acc = x
    for c in range(D_FF // FF_CHUNK):
        cols = slice(c * FF_CHUNK, (c + 1) * FF_CHUNK)
        a = jnp.maximum(_dot(h, w1_ref[:, cols]), 0.0)
        acc = acc + _dot((a * a).astype(BF16), w2_ref[cols, :])
    if with_final_norm:
        acc = _rmsnorm(acc, gf_ref[...])
    o_ref[...] = acc


def _mlp(x, g, w1, w2, *, layer, tm, attn=None, w_o=None, w_o_layer=None, final_g=None):
    n, d = x.shape
    with_proj = attn is not None
    with_final_norm = final_g is not None
    row = pl.BlockSpec((tm, d), lambda i: (i, 0))
    args, specs = [x], [row]
    if with_proj:
        args += [attn, w_o]
        specs += [row, _resident((d, d), w_o_layer)]
    args += [g, w1, w2]
    specs += [_resident((1, d), layer), _resident((d, D_FF), layer),
              _resident((D_FF, d), layer)]
    if with_final_norm:
        args.append(final_g)
        specs.append(_resident((1, d)))
    kernel = functools.partial(_mlp_kernel, with_proj=with_proj,
                               with_final_norm=with_final_norm)
    return pl.pallas_call(
        kernel,
        grid=(n // tm,),
        in_specs=specs,
        out_specs=row,
        out_shape=jax.ShapeDtypeStruct((n, d), F32),
        compiler_params=_params("arbitrary"),
        name="mlp",
    )(*args)


def _qkv_kernel(x_ref, g_ref, w_qkv_ref, qkv_ref):
    d = D_MODEL
    h = _rmsnorm(x_ref[...], g_ref[...]).astype(BF16)
    scale = 1.0 / math.sqrt(HEAD_DIM)
    qkv_ref[:, 0:d] = (_dot(h, w_qkv_ref[:, 0:d]) * scale).astype(BF16)
    qkv_ref[:, d:3 * d] = _dot(h, w_qkv_ref[:, d:3 * d]).astype(BF16)


def _qkv(x, g, w_qkv, *, layer, tm):
    n, d = x.shape
    return pl.pallas_call(
        _qkv_kernel,
        grid=(n // tm,),
        in_specs=[pl.BlockSpec((tm, d), lambda i: (i, 0)), _resident((1, d), layer),
                  _resident((d, 3 * d), layer)],
        out_specs=pl.BlockSpec((tm, 3 * d), lambda i: (i, 0)),
        out_shape=jax.ShapeDtypeStruct((n, 3 * d), BF16),
        compiler_params=_params("arbitrary"),
        name="qkv_proj",
    )(x, g, w_qkv)


def _suffix_sum_matrix(t, sign):
    r = jnp.arange(t)[:, None]
    c = jnp.arange(t + LANES)[None, :]
    return jnp.where((c >= t) | (r > c), sign, 0.0).astype(BF16)


def _head_of_lane(shape):
    return lax.broadcasted_iota(jnp.int32, shape, len(shape) - 1) // HEAD_DIM


def _attn_small_kernel(q_ref, k_ref, v_ref, u_ref, o_ref, *, t):
    q = q_ref[...]
    k = k_ref[...]
    head_of_lane = _head_of_lane((t, LANES))
    row_i = lax.broadcasted_iota(jnp.int32, (t, t), 0)
    col_i = lax.broadcasted_iota(jnp.int32, (t, t), 1)
    causal = col_i < row_i
    out = None
    for hh in range(HEADS_PER_GROUP):
        qh = jnp.where(head_of_lane == hh, q, jnp.zeros_like(q))
        z = lax.dot_general(qh, k, (((1,), (1,)), ((), ())), preferred_element_type=F32)
        soft = jnp.log(1.0 + jnp.exp(-jnp.abs(z)))
        log_beta = jnp.minimum(z, 0.0) - soft
        log_1m = jnp.where(causal, log_beta - z, 0.0)
        tail = _dot(log_1m.astype(BF16), u_ref[:, 0:t])
        w = jnp.where(causal, jnp.exp(log_beta + tail), 0.0)
        oh = _dot(w.astype(BF16), v_ref[...])
        out = oh if out is None else jnp.where(head_of_lane == hh, oh, out)
    o_ref[...] = out.astype(BF16)


def _attention_small(qkv):
    t = qkv.shape[0]
    d = D_MODEL
    u = _suffix_sum_matrix(t, 1.0)
    col0 = d // LANES
    return pl.pallas_call(
        functools.partial(_attn_small_kernel, t=t),
        grid=(N_GROUPS,),
        in_specs=[pl.BlockSpec((t, LANES), lambda g: (0, g)),
                  pl.BlockSpec((t, LANES), lambda g: (0, col0 + g)),
                  pl.BlockSpec((t, LANES), lambda g: (0, 2 * col0 + g)),
                  _resident(u.shape)],
        out_specs=pl.BlockSpec((t, LANES), lambda g: (0, g)),
        out_shape=jax.ShapeDtypeStruct((t, d), BF16),
        compiler_params=_params("arbitrary"),
        name="attention_meta",
    )(qkv, qkv, qkv, u)


_KIND_PLAIN, _KIND_DIAG, _KIND_META = 0, 1, 2
_PIPE_DEPTH = 4
_UNROLL = 12
_ITEM_FIELDS = 6
_PIPELINED_BLOCKS = 2
assert _PIPELINED_BLOCKS == 2 and _PIPE_DEPTH % 2 == 0 and _UNROLL % 2 == 0
STEP_GROUPS = 2
LOG_UNDERFLOW = -105.0


def _tile_blocks(qi, s, t):
    blocks = [(qi * t, _KIND_DIAG)]
    blocks += [(j * t, _KIND_PLAIN) for j in range(qi - 1, -1, -1)]
    blocks.append((s, _KIND_META))
    return blocks


def _attention_items(s, t):
    n_tiles = s // t
    idle = [s, 0, _KIND_PLAIN, STEP_GROUPS * n_tiles, 0, 0]
    items = []
    for grp in range(STEP_GROUPS):
        for qi in range(n_tiles):
            for k_off, kind in _tile_blocks(qi, s, t)[:_PIPELINED_BLOCKS]:
                items.append([qi * t, k_off, kind, grp * n_tiles + qi, grp, qi * t])
    lead = _PIPE_DEPTH - 1
    n_iter = len(items) + lead
    n_iter += -n_iter % _UNROLL
    table = [idle] * lead + items
    table += [idle] * (n_iter + lead - len(table))
    return np.asarray(table, np.int32).reshape(-1), n_iter


def _mask_bias(t):
    r = np.arange(t)[:, None]
    c = np.arange(t)[None, :]
    bias = np.zeros((3, t, t), np.float32)
    bias[_KIND_DIAG] = np.where(c < r, 0.0, MASK_BIAS)
    bias[_KIND_META] = np.where(c < N_META, 0.0, MASK_BIAS) + 0.0 * r
    return jnp.asarray(bias)


def _attn_pipe_kernel(tab_ref, q_ref, k_ref, v_ref, km_ref, vm_ref, u_ref, bias_ref,
                      o_ref, kt_all, v_all, zbuf, lbuf, pbuf, wbuf, carry_ref,
                      acc_ref, carry_all, acc_all, obuf, open_ref, *, s, t, n_iter):
    nh = HEADS_PER_GROUP
    first_step = jnp.logical_and(pl.program_id(0) == 0, pl.program_id(1) == 0)

    @pl.when(first_step)
    def _():
        for ref in (zbuf, lbuf, pbuf, wbuf, carry_ref, acc_ref):
            ref[...] = jnp.zeros_like(ref)
        kt_all[...] = jnp.zeros_like(kt_all)
        v_all[:, s + META_ROWS:s + t, :] = jnp.zeros(
            (STEP_GROUPS * nh, t - META_ROWS, LANES), BF16)

    head_of_lane = _head_of_lane((1, LANES))
    for grp in range(STEP_GROUPS):
        cols = slice(grp * LANES, (grp + 1) * LANES)
        v = v_ref[:, cols]
        vm = vm_ref[:, cols]
        for hh in range(nh):
            mine = head_of_lane == hh
            v_all[grp * nh + hh, 0:s, :] = jnp.where(mine, v, jnp.zeros_like(v))
            v_all[grp * nh + hh, s:s + META_ROWS, :] = jnp.where(mine, vm, jnp.zeros_like(vm))
        chunks = [(k_ref, c, c) for c in range(0, s, t)] + [(km_ref, 0, s)]
        for src_ref, src_row, dst_col in chunks:
            n_keys = min(t, src_ref.shape[0])
            kt = src_ref[src_row:src_row + n_keys, cols].T
            for hh in range(nh):
                rows = slice(hh * HEAD_DIM, (hh + 1) * HEAD_DIM)
                kt_all[grp * nh + hh, rows, dst_col:dst_col + n_keys] = kt[rows, :]

    def stage_scores(q_off, k_off, grp, p):
        q_blk = q_ref[pl.ds(pl.multiple_of(q_off, t), t),
                      pl.ds(pl.multiple_of(grp * LANES, LANES), LANES)]
        for hh in range(nh):
            zbuf[p, hh] = _dot(q_blk, kt_all[grp * nh + hh, :, pl.ds(pl.multiple_of(k_off, t), t)])

    def stage_logs(kind, p):
        bias = bias_ref[kind]
        for hh in range(nh):
            z = zbuf[1 - p, hh] + bias
            soft = jnp.log(1.0 + jnp.exp2(jnp.abs(z) * (-LOG2_E)))
            neg_log_1m = jnp.maximum(z, 0.0) + soft
            lbuf[p, hh] = z - neg_log_1m
            pbuf[p, hh] = neg_log_1m.astype(BF16)

    def stage_weights(first, p):
        for hh in range(nh):
            sums = _dot(pbuf[1 - p, hh], u_ref[...])
            tail, row_sum = sums[:, 0:t], sums[:, t:t + LANES]
            if first:
                carry_ref[hh] = row_sum
            else:
                carry = carry_ref[hh]
                tail = tail + jnp.concatenate([carry] * (t // LANES), axis=1)
                carry_ref[hh] = carry + row_sum
            wbuf[p, hh] = jnp.exp(lbuf[1 - p, hh] + tail).astype(BF16)

    def stage_output(q_off, k_off, grp, first, p):
        k_off = pl.multiple_of(k_off, t)
        w_cat = jnp.concatenate([wbuf[1 - p, hh] for hh in range(nh)], axis=1)
        v_cat = jnp.concatenate([v_all[grp * nh + hh, pl.ds(k_off, t), :]
                                 for hh in range(nh)], axis=0)
        acc = _dot(w_cat, v_cat)
        if not first:
            acc = acc_ref[...] + acc
            obuf[grp, pl.ds(pl.multiple_of(q_off, t), t), :] = acc.astype(BF16)
        acc_ref[...] = acc

    def field(item, f):
        return tab_ref[item * _ITEM_FIELDS + f]

    def iteration(i, parity):
        p = parity
        stage_logs(field(i + 2, 2), p)
        if parity == 0:
            stage_weights(True, p)
        else:
            stage_weights(False, p)
            carry_all[field(i + 1, 3)] = carry_ref[...]
        stage_scores(field(i + 3, 5), field(i + 3, 1), field(i + 3, 4), p)
        stage_output(field(i, 0), field(i, 1), field(i, 4), parity == 1, p)
        if parity == 0:
            acc_all[field(i, 3)] = acc_ref[...]

    def body(k, c):
        for j in range(_UNROLL):
            iteration(_UNROLL * k + j, j % 2)
        return c

    lax.fori_loop(0, n_iter // _UNROLL, body, 0)

    def saturated():
        return jnp.max(carry_ref[...]) <= LOG_UNDERFLOW

    n_tiles = s // t
    first_open = _PIPELINED_BLOCKS - 1

    def finish_tile(slot, c):
        grp = slot // n_tiles
        qi = slot % n_tiles
        n_blocks = qi + 2

        @pl.when(open_ref[slot] != 0)
        def _():
            carry_ref[...] = carry_all[slot]
            acc_ref[...] = acc_all[slot]
            q_off = qi * t

            def more(state):
                n, done = state
                return jnp.logical_and(n < n_blocks, jnp.logical_not(done))

            def one_block(state):
                n, _ = state
                is_meta = n == n_blocks - 1
                k_off = jnp.where(is_meta, s, (qi - n) * t)
                kind = jnp.where(is_meta, _KIND_META, _KIND_PLAIN)
                stage_scores(q_off, k_off, grp, 0)
                stage_logs(kind, 1)
                stage_weights(None, 0)
                stage_output(q_off, k_off, grp, None, 1)
                return n + 1, saturated()

            lax.while_loop(more, one_block, (jnp.int32(_PIPELINED_BLOCKS), jnp.bool_(False)))

        return c

    open_slots = [grp * n_tiles + qi for grp in range(STEP_GROUPS)
                  for qi in range(first_open, n_tiles)]
    worst = carry_all[open_slots[0]]
    for slot in open_slots[1:]:
        worst = jnp.maximum(worst, carry_all[slot])

    @pl.when(jnp.max(worst) > LOG_UNDERFLOW)
    def _():
        for slot in range(STEP_GROUPS * n_tiles):
            if slot in open_slots:
                open_ref[slot] = (jnp.max(carry_all[slot]) > LOG_UNDERFLOW).astype(jnp.int32)
            else:
                open_ref[slot] = 0
        lax.fori_loop(0, STEP_GROUPS * n_tiles, finish_tile, 0)

    for grp in range(STEP_GROUPS):
        o_ref[:, grp * LANES:(grp + 1) * LANES] = obuf[grp, 0:s, :]


def _attention(qkv, meta_qkv, *, t):
    b, s, _ = qkv.shape
    d = D_MODEL
    nh = HEADS_PER_GROUP
    table, n_iter = _attention_items(s, t)
    u = _suffix_sum_matrix(t, -1.0)
    bias = _mask_bias(t)
    width = STEP_GROUPS * LANES
    col0 = d // width
    n_slots = STEP_GROUPS * (s // t) + 1
    grid_spec = pltpu.PrefetchScalarGridSpec(
        num_scalar_prefetch=1,
        grid=(b, N_GROUPS // STEP_GROUPS),
        in_specs=[
            pl.BlockSpec((None, s, width), lambda bi, g, tab: (bi, 0, g)),
            pl.BlockSpec((None, s, width), lambda bi, g, tab: (bi, 0, col0 + g)),
            pl.BlockSpec((None, s, width), lambda bi, g, tab: (bi, 0, 2 * col0 + g)),
            pl.BlockSpec((META_ROWS, width), lambda bi, g, tab: (0, col0 + g)),
            pl.BlockSpec((META_ROWS, width), lambda bi, g, tab: (0, 2 * col0 + g)),
            pl.BlockSpec(u.shape, lambda bi, g, tab: (0, 0)),
            pl.BlockSpec(bias.shape, lambda bi, g, tab: (0, 0, 0)),
        ],
        out_specs=pl.BlockSpec((None, s, width), lambda bi, g, tab: (bi, 0, g)),
        scratch_shapes=[
            pltpu.VMEM((STEP_GROUPS * nh, LANES, s + t), BF16),
            pltpu.VMEM((STEP_GROUPS * nh, s + t, LANES), BF16),
            pltpu.VMEM((2, nh, t, t), F32),
            pltpu.VMEM((2, nh, t, t), F32),
            pltpu.VMEM((2, nh, t, t), BF16),
            pltpu.VMEM((2, nh, t, t), BF16),
            pltpu.VMEM((nh, t, LANES), F32),
            pltpu.VMEM((t, LANES), F32),
            pltpu.VMEM((n_slots, nh, t, LANES), F32),
            pltpu.VMEM((n_slots, t, LANES), F32),
            pltpu.VMEM((STEP_GROUPS, s + t, LANES), BF16),
            pltpu.SMEM((n_slots,), jnp.int32),
        ],
    )
    kernel = functools.partial(_attn_pipe_kernel, s=s, t=t, n_iter=n_iter)
    return pl.pallas_call(
        kernel,
        grid_spec=grid_spec,
        out_shape=jax.ShapeDtypeStruct((b, s, d), BF16),
        compiler_params=_params("arbitrary", "arbitrary"),
        name="stick_breaking_attention",
    )(jnp.asarray(table), qkv, qkv, qkv, meta_qkv, meta_qkv, u, bias)


def kernel(x, meta_tokens, conv_norm, conv_w_in, conv_w, conv_w_out, attn_norm,
           attn_w_qkv, attn_w_out, mlp_norm, mlp_w1, mlp_w2, final_norm):
    b, s, d = x.shape
    depth = mlp_norm.shape[0]
    hr = x.reshape(b * s, d)
    hm = jnp.pad(meta_tokens.astype(x.dtype), ((0, META_ROWS - N_META), (0, 0)))
    zero_hist = jnp.zeros((SUBLANES, d), F32)
    rows = lambda g: g.reshape(-1, 1, d)

    w1, w2 = mlp_w1.astype(BF16), mlp_w2.astype(BF16)
    w_in, w_out = conv_w_in.astype(BF16), conv_w_out.astype(BF16)
    w_qkv, w_o = attn_w_qkv.astype(BF16), attn_w_out.astype(BF16)
    g_conv, g_attn, g_mlp = rows(conv_norm), rows(attn_norm), rows(mlp_norm)
    g_final = final_norm.reshape(1, d)

    for i in range(depth):
        j = i // 2
        last = i == depth - 1
        final_g = g_final if last else None
        if i % 2 == 0:
            hm, hist = _conv_mixer(hm, g_conv, w_in, conv_w, w_out, zero_hist, layer=j,
                                   tm=META_ROWS, tiles_per_seq=1, hist_row=N_META)
            hr = _conv_mixer(hr, g_conv, w_in, conv_w, w_out, hist, layer=j,
                             tm=ROW_TILE, tiles_per_seq=s // ROW_TILE)
            hm = _mlp(hm, g_mlp, w1, w2, layer=i, tm=META_ROWS)
            hr = _mlp(hr, g_mlp, w1, w2, layer=i, tm=ROW_TILE, final_g=final_g)
        else:
            qkv_m = _qkv(hm, g_attn, w_qkv, layer=j, tm=META_ROWS)
            qkv_r = _qkv(hr, g_attn, w_qkv, layer=j, tm=ROW_TILE)
            o_r = _attention(qkv_r.reshape(b, s, 3 * d), qkv_m, t=ATTN_TILE)
            if not last:
                o_m = _attention_small(qkv_m)
                hm = _mlp(hm, g_mlp, w1, w2, layer=i, tm=META_ROWS,
                          attn=o_m, w_o=w_o, w_o_layer=j)
            hr = _mlp(hr, g_mlp, w1, w2, layer=i, tm=ROW_TILE,
                      attn=o_r.reshape(b * s, d), w_o=w_o, w_o_layer=j, final_g=final_g)
    return hr.reshape(b, s, d)
```

```python
import functools
import math

import numpy as np
import jax
import jax.numpy as jnp
from jax import lax
from jax.experimental import pallas as pl
from jax.experimental.pallas import tpu as pltpu

D_MODEL = 1024
N_META = 16
N_HEADS = 16
HEAD_DIM = D_MODEL // N_HEADS
CONV_W = 3
D_FF = 4 * D_MODEL
RMS_EPS = 1e-6

LANES = 128
SUBLANES = 8
HEADS_PER_GROUP = LANES // HEAD_DIM
N_GROUPS = N_HEADS // HEADS_PER_GROUP
META_ROWS = LANES
ROW_TILE = 512
ATTN_TILE = 256
FF_CHUNK = 1024
VMEM_LIMIT_BYTES = 56 * 1024 * 1024
LOG2_E = math.log2(math.e)
MASK_BIAS = -1e30

BF16 = jnp.bfloat16
F32 = jnp.float32


def _dot(a, b):
    return jnp.dot(a, b, preferred_element_type=F32)


def _rmsnorm(x, g):
    return x * lax.rsqrt(jnp.mean(x * x, axis=-1, keepdims=True) + RMS_EPS) * g


def _resident(shape, layer=None):
    zeros = (0,) * len(shape)
    if layer is None:
        return pl.BlockSpec(shape, lambda *_: zeros, pipeline_mode=pl.Buffered(1))
    return pl.BlockSpec((None,) + tuple(shape), lambda *_: (layer,) + zeros,
                        pipeline_mode=pl.Buffered(1))


def _params(*semantics, flags=None):
    return pltpu.CompilerParams(dimension_semantics=semantics,
                                vmem_limit_bytes=VMEM_LIMIT_BYTES, flags=flags)


def _conv_mixer_kernel(x_ref, g_ref, w_in_ref, cw_ref, w_out_ref, hist_ref,
                       *rest, tm, tiles_per_seq, hist_row):
    if hist_row is None:
        o_ref, ubuf = rest
    else:
        o_ref, hist_out_ref, ubuf = rest
    i = pl.program_id(0)
    first = (i % tiles_per_seq) == 0

    @pl.when(first)
    def _():
        ubuf[0:SUBLANES, :] = hist_ref[...]

    @pl.when(jnp.logical_not(first))
    def _():
        ubuf[0:SUBLANES, :] = ubuf[tm:tm + SUBLANES, :]

    x = x_ref[...]
    h = _rmsnorm(x, g_ref[...]).astype(BF16)
    d = D_MODEL
    gate_c = _dot(h, w_in_ref[:, d:2 * d])
    val = _dot(h, w_in_ref[:, 2 * d:3 * d])
    ubuf[SUBLANES:SUBLANES + tm, :] = gate_c * val
    if hist_row is not None:
        hist_out_ref[...] = ubuf[hist_row:hist_row + SUBLANES, :]
    cw = cw_ref[...]
    conv = ubuf[SUBLANES:SUBLANES + tm, :] * cw[CONV_W - 1:CONV_W, :]
    for k in range(CONV_W - 1):
        back = CONV_W - 1 - k
        conv = conv + ubuf[SUBLANES - back:SUBLANES - back + tm, :] * cw[k:k + 1, :]
    gate_b = _dot(h, w_in_ref[:, 0:d])
    y = (gate_b * conv).astype(BF16)
    o_ref[...] = x + _dot(y, w_out_ref[...])


def _conv_mixer(x, g, w_in, cw, w_out, hist, *, layer, tm, tiles_per_seq, hist_row=None):
    n, d = x.shape
    kernel = functools.partial(_conv_mixer_kernel, tm=tm,
                               tiles_per_seq=tiles_per_seq, hist_row=hist_row)
    row = pl.BlockSpec((tm, d), lambda i: (i, 0))
    out_shape = jax.ShapeDtypeStruct((n, d), F32)
    out_specs = row
    if hist_row is not None:
        out_shape = (out_shape, jax.ShapeDtypeStruct((SUBLANES, d), F32))
        out_specs = (row, pl.BlockSpec((SUBLANES, d), lambda i: (0, 0)))
    return pl.pallas_call(
        kernel,
        grid=(n // tm,),
        in_specs=[row, _resident((1, d), layer), _resident((d, 3 * d), layer),
                  _resident((CONV_W, d), layer), _resident((d, d), layer),
                  _resident((SUBLANES, d))],
        out_specs=out_specs,
        out_shape=out_shape,
        scratch_shapes=[pltpu.VMEM((tm + 2 * SUBLANES, d), F32)],
        compiler_params=_params("arbitrary"),
        name="conv_mixer",
    )(x, g, w_in, cw, w_out, hist)


def _mlp_kernel(*refs, with_proj, with_final_norm):
    refs = list(refs)
    x_ref = refs.pop(0)
    if with_proj:
        a_ref, w_o_ref = refs.pop(0), refs.pop(0)
    g_ref, w1_ref, w2_ref = refs.pop(0), refs.pop(0), refs.pop(0)
    if with_final_norm:
        gf_ref = refs.pop(0)
    (o_ref,) = refs

    x = x_ref[...]
    if with_proj:
        x = x + _dot(a_ref[...], w_o_ref[...])
    h = _rmsnorm(x, g_ref[...]).astype(BF16)
    acc = x
    for c in range(D_FF // FF_CHUNK):
        cols = slice(c * FF_CHUNK, (c + 1) * FF_CHUNK)
        a = jnp.maximum(_dot(h, w1_ref[:, cols]), 0.0)
        acc = acc + _dot((a * a).astype(BF16), w2_ref[cols, :])
    if with_final_norm:
        acc = _rmsnorm(acc, gf_ref[...])
    o_ref[...] = acc


def _mlp(x, g, w1, w2, *, layer, tm, attn=None, w_o=None, w_o_layer=None, final_g=None):
    n, d = x.shape
    with_proj = attn is not None
    with_final_norm = final_g is not None
    row = pl.BlockSpec((tm, d), lambda i: (i, 0))
    args, specs = [x], [row]
    if with_proj:
        args += [attn, w_o]
        specs += [row, _resident((d, d), w_o_layer)]
    args += [g, w1, w2]
    specs += [_resident((1, d), layer), _resident((d, D_FF), layer),
              _resident((D_FF, d), layer)]
    if with_final_norm:
        args.append(final_g)
        specs.append(_resident((1, d)))
    kernel = functools.partial(_mlp_kernel, with_proj=with_proj,
                               with_final_norm=with_final_norm)
    return pl.pallas_call(
        kernel,
        grid=(n // tm,),
        in_specs=specs,
        out_specs=row,
        out_shape=jax.ShapeDtypeStruct((n, d), F32),
        compiler_params=_params("arbitrary"),
        name="mlp",
    )(*args)


def _qkv_kernel(x_ref, g_ref, w_qkv_ref, qkv_ref):
    d = D_MODEL
    h = _rmsnorm(x_ref[...], g_ref[...]).astype(BF16)
    scale = 1.0 / math.sqrt(HEAD_DIM)
    qkv_ref[:, 0:d] = (_dot(h, w_qkv_ref[:, 0:d]) * scale).astype(BF16)
    qkv_ref[:, d:3 * d] = _dot(h, w_qkv_ref[:, d:3 * d]).astype(BF16)


def _qkv(x, g, w_qkv, *, layer, tm):
    n, d = x.shape
    return pl.pallas_call(
        _qkv_kernel,
        grid=(n // tm,),
        in_specs=[pl.BlockSpec((tm, d), lambda i: (i, 0)), _resident((1, d), layer),
                  _resident((d, 3 * d), layer)],
        out_specs=pl.BlockSpec((tm, 3 * d), lambda i: (i, 0)),
        out_shape=jax.ShapeDtypeStruct((n, 3 * d), BF16),
        compiler_params=_params("arbitrary"),
        name="qkv_proj",
    )(x, g, w_qkv)


def _suffix_sum_matrix(t, sign):
    r = jnp.arange(t)[:, None]
    c = jnp.arange(t + LANES)[None, :]
    return jnp.where((c >= t) | (r > c), sign, 0.0).astype(BF16)


def _head_of_lane(shape):
    return lax.broadcasted_iota(jnp.int32, shape, len(shape) - 1) // HEAD_DIM


def _attn_small_kernel(q_ref, k_ref, v_ref, u_ref, o_ref, *, t):
    q = q_ref[...]
    k = k_ref[...]
    head_of_lane = _head_of_lane((t, LANES))
    row_i = lax.broadcasted_iota(jnp.int32, (t, t), 0)
    col_i = lax.broadcasted_iota(jnp.int32, (t, t), 1)
    causal = col_i < row_i
    out = None
    for hh in range(HEADS_PER_GROUP):
        qh = jnp.where(head_of_lane == hh, q, jnp.zeros_like(q))
        z = lax.dot_general(qh, k, (((1,), (1,)), ((), ())), preferred_element_type=F32)
        soft = jnp.log(1.0 + jnp.exp(-jnp.abs(z)))
        log_beta = jnp.minimum(z, 0.0) - soft
        log_1m = jnp.where(causal, log_beta - z, 0.0)
        tail = _dot(log_1m.astype(BF16), u_ref[:, 0:t])
        w = jnp.where(causal, jnp.exp(log_beta + tail), 0.0)
        oh = _dot(w.astype(BF16), v_ref[...])
        out = oh if out is None else jnp.where(head_of_lane == hh, oh, out)
    o_ref[...] = out.astype(BF16)


def _attention_small(qkv):
    t = qkv.shape[0]
    d = D_MODEL
    u = _suffix_sum_matrix(t, 1.0)
    col0 = d // LANES
    return pl.pallas_call(
        functools.partial(_attn_small_kernel, t=t),
        grid=(N_GROUPS,),
        in_specs=[pl.BlockSpec((t, LANES), lambda g: (0, g)),
                  pl.BlockSpec((t, LANES), lambda g: (0, col0 + g)),
                  pl.BlockSpec((t, LANES), lambda g: (0, 2 * col0 + g)),
                  _resident(u.shape)],
        out_specs=pl.BlockSpec((t, LANES), lambda g: (0, g)),
        out_shape=jax.ShapeDtypeStruct((t, d), BF16),
        compiler_params=_params("arbitrary"),
        name="attention_meta",
    )(qkv, qkv, qkv, u)


_KIND_PLAIN, _KIND_DIAG, _KIND_META = 0, 1, 2
_PIPE_DEPTH = 4
_UNROLL = 12
_ITEM_FIELDS = 6
_PIPELINED_BLOCKS = 2
assert _PIPELINED_BLOCKS == 2 and _PIPE_DEPTH % 2 == 0 and _UNROLL % 2 == 0
STEP_GROUPS = 2
LOG_UNDERFLOW = -105.0


def _tile_blocks(qi, s, t):
    blocks = [(qi * t, _KIND_DIAG)]
    blocks += [(j * t, _KIND_PLAIN) for j in range(qi - 1, -1, -1)]
    blocks.append((s, _KIND_META))
    return blocks


def _attention_items(s, t):
    n_tiles = s // t
    items = []
    for grp in range(STEP_GROUPS):
        for qi in range(n_tiles):
            for k_off, kind in _tile_blocks(qi, s, t)[:_PIPELINED_BLOCKS]:
                items.append([qi * t, k_off, kind, grp * n_tiles + qi, grp, qi * t])
    lead = _PIPE_DEPTH - 1
    n_iter = len(items) + lead
    n_iter += -n_iter % _UNROLL
    first_pair, last_pair = items[:2], items[-2:]
    table = (first_pair * 2)[-lead:] + items
    while len(table) < n_iter + lead:
        table.append(last_pair[(len(table) - lead) % 2])
    return np.asarray(table, np.int32).reshape(-1), n_iter


def _mask_bias(t):
    r = np.arange(t)[:, None]
    c = np.arange(t)[None, :]
    bias = np.zeros((3, t, t), np.float32)
    bias[_KIND_DIAG] = np.where(c < r, 0.0, MASK_BIAS)
    bias[_KIND_META] = np.where(c < N_META, 0.0, MASK_BIAS) + 0.0 * r
    return jnp.asarray(bias)


def _attn_pipe_kernel(tab_ref, q_ref, k_ref, v_ref, km_ref, vm_ref, u_ref, bias_ref,
                      o_ref, kt_all, v_all, zbuf, lbuf, pbuf, wbuf, carry_ref,
                      acc_ref, carry_all, acc_all, open_ref, *, s, t, n_iter):
    nh = HEADS_PER_GROUP
    first_step = jnp.logical_and(pl.program_id(0) == 0, pl.program_id(1) == 0)

    @pl.when(first_step)
    def _():
        for ref in (zbuf, lbuf, pbuf, wbuf, carry_ref, acc_ref):
            ref[...] = jnp.zeros_like(ref)
        kt_all[...] = jnp.zeros_like(kt_all)
        v_all[:, s + META_ROWS:s + t, :] = jnp.zeros(
            (STEP_GROUPS * nh, t - META_ROWS, LANES), BF16)

    head_of_lane = _head_of_lane((1, LANES))
    for grp in range(STEP_GROUPS):
        cols = slice(grp * LANES, (grp + 1) * LANES)
        v = v_ref[:, cols]
        vm = vm_ref[:, cols]
        for hh in range(nh):
            mine = head_of_lane == hh
            v_all[grp * nh + hh, 0:s, :] = jnp.where(mine, v, jnp.zeros_like(v))
            v_all[grp * nh + hh, s:s + META_ROWS, :] = jnp.where(mine, vm, jnp.zeros_like(vm))
        chunks = [(k_ref, c, c) for c in range(0, s, t)] + [(km_ref, 0, s)]
        for src_ref, src_row, dst_col in chunks:
            n_keys = min(t, src_ref.shape[0])
            kt = src_ref[src_row:src_row + n_keys, cols].T
            for hh in range(nh):
                rows = slice(hh * HEAD_DIM, (hh + 1) * HEAD_DIM)
                kt_all[grp * nh + hh, rows, dst_col:dst_col + n_keys] = kt[rows, :]

    def stage_scores(q_off, k_off, grp, p):
        q_blk = q_ref[pl.ds(pl.multiple_of(q_off, t), t),
                      pl.ds(pl.multiple_of(grp * LANES, LANES), LANES)]
        for hh in range(nh):
            zbuf[p, hh] = _dot(q_blk, kt_all[grp * nh + hh, :, pl.ds(pl.multiple_of(k_off, t), t)])

    def stage_logs(kind, p):
        bias = bias_ref[kind]
        for hh in range(nh):
            z = zbuf[1 - p, hh] + bias
            soft = jnp.log(1.0 + jnp.exp2(jnp.abs(z) * (-LOG2_E)))
            neg_log_1m = jnp.maximum(z, 0.0) + soft
            lbuf[p, hh] = z - neg_log_1m
            pbuf[p, hh] = neg_log_1m.astype(BF16)

    def stage_weights(first, p):
        for hh in range(nh):
            sums = _dot(pbuf[1 - p, hh], u_ref[...])
            tail, row_sum = sums[:, 0:t], sums[:, t:t + LANES]
            if first:
                carry_ref[hh] = row_sum
            else:
                carry = carry_ref[hh]
                tail = tail + jnp.concatenate([carry] * (t // LANES), axis=1)
                carry_ref[hh] = carry + row_sum
            wbuf[p, hh] = jnp.exp(lbuf[1 - p, hh] + tail).astype(BF16)

    def stage_output(q_off, k_off, grp, first, p):
        k_off = pl.multiple_of(k_off, t)
        w_cat = jnp.concatenate([wbuf[1 - p, hh] for hh in range(nh)], axis=1)
        v_cat = jnp.concatenate([v_all[grp * nh + hh, pl.ds(k_off, t), :]
                                 for hh in range(nh)], axis=0)
        acc = _dot(w_cat, v_cat)
        if not first:
            acc = acc_ref[...] + acc
            o_ref[pl.ds(pl.multiple_of(q_off, t), t),
                  pl.ds(pl.multiple_of(grp * LANES, LANES), LANES)] = acc.astype(BF16)
        acc_ref[...] = acc

    def field(item, f):
        return tab_ref[item * _ITEM_FIELDS + f]

    def iteration(i, parity):
        p = parity
        stage_logs(field(i + 2, 2), p)
        if parity == 0:
            stage_weights(True, p)
        else:
            stage_weights(False, p)
            carry_all[field(i + 1, 3)] = carry_ref[...]
        stage_scores(field(i + 3, 5), field(i + 3, 1), field(i + 3, 4), p)
        stage_output(field(i, 0), field(i, 1), field(i, 4), parity == 1, p)
        if parity == 0:
            acc_all[field(i, 3)] = acc_ref[...]

    def body(k, c):
        for j in range(_UNROLL):
            iteration(_UNROLL * k + j, j % 2)
        return c

    lax.fori_loop(0, n_iter // _UNROLL, body, 0)

    def saturated():
        return jnp.max(carry_ref[...]) <= LOG_UNDERFLOW

    n_tiles = s // t
    first_open = _PIPELINED_BLOCKS - 1

    def finish_tile(slot, c):
        grp = slot // n_tiles
        qi = slot % n_tiles
        n_blocks = qi + 2

        @pl.when(open_ref[slot] != 0)
        def _():
            carry_ref[...] = carry_all[slot]
            acc_ref[...] = acc_all[slot]
            q_off = qi * t

            def more(state):
                n, done = state
                return jnp.logical_and(n < n_blocks, jnp.logical_not(done))

            def one_block(state):
                n, _ = state
                is_meta = n == n_blocks - 1
                k_off = jnp.where(is_meta, s, (qi - n) * t)
                kind = jnp.where(is_meta, _KIND_META, _KIND_PLAIN)
                stage_scores(q_off, k_off, grp, 0)
                stage_logs(kind, 1)
                stage_weights(None, 0)
                stage_output(q_off, k_off, grp, None, 1)
                return n + 1, saturated()

            lax.while_loop(more, one_block, (jnp.int32(_PIPELINED_BLOCKS), jnp.bool_(False)))

        return c

    open_slots = [grp * n_tiles + qi for grp in range(STEP_GROUPS)
                  for qi in range(first_open, n_tiles)]
    worst = carry_all[open_slots[0]]
    for slot in open_slots[1:]:
        worst = jnp.maximum(worst, carry_all[slot])

    @pl.when(jnp.max(worst) > LOG_UNDERFLOW)
    def _():
        for slot in range(STEP_GROUPS * n_tiles):
            if slot in open_slots:
                open_ref[slot] = (jnp.max(carry_all[slot]) > LOG_UNDERFLOW).astype(jnp.int32)
            else:
                open_ref[slot] = 0
        lax.fori_loop(0, STEP_GROUPS * n_tiles, finish_tile, 0)


def _attention(qkv, meta_qkv, *, t):
    b, s, _ = qkv.shape
    d = D_MODEL
    nh = HEADS_PER_GROUP
    table, n_iter = _attention_items(s, t)
    u = _suffix_sum_matrix(t, -1.0)
    bias = _mask_bias(t)
    width = STEP_GROUPS * LANES
    col0 = d // width
    n_slots = STEP_GROUPS * (s // t)
    grid_spec = pltpu.PrefetchScalarGridSpec(
        num_scalar_prefetch=1,
        grid=(b, N_GROUPS // STEP_GROUPS),
        in_specs=[
            pl.BlockSpec((None, s, width), lambda bi, g, tab: (bi, 0, g)),
            pl.BlockSpec((None, s, width), lambda bi, g, tab: (bi, 0, col0 + g)),
            pl.BlockSpec((None, s, width), lambda bi, g, tab: (bi, 0, 2 * col0 + g)),
            pl.BlockSpec((META_ROWS, width), lambda bi, g, tab: (0, col0 + g)),
            pl.BlockSpec((META_ROWS, width), lambda bi, g, tab: (0, 2 * col0 + g)),
            pl.BlockSpec(u.shape, lambda bi, g, tab: (0, 0)),
            pl.BlockSpec(bias.shape, lambda bi, g, tab: (0, 0, 0)),
        ],
        out_specs=pl.BlockSpec((None, s, width), lambda bi, g, tab: (bi, 0, g)),
        scratch_shapes=[
            pltpu.VMEM((STEP_GROUPS * nh, LANES, s + t), BF16),
            pltpu.VMEM((STEP_GROUPS * nh, s + t, LANES), BF16),
            pltpu.VMEM((2, nh, t, t), F32),
            pltpu.VMEM((2, nh, t, t), F32),
            pltpu.VMEM((2, nh, t, t), BF16),
            pltpu.VMEM((2, nh, t, t), BF16),
            pltpu.VMEM((nh, t, LANES), F32),
            pltpu.VMEM((t, LANES), F32),
            pltpu.VMEM((n_slots, nh, t, LANES), F32),
            pltpu.VMEM((n_slots, t, LANES), F32),
            pltpu.SMEM((n_slots,), jnp.int32),
        ],
    )
    kernel = functools.partial(_attn_pipe_kernel, s=s, t=t, n_iter=n_iter)
    return pl.pallas_call(
        kernel,
        grid_spec=grid_spec,
        out_shape=jax.ShapeDtypeStruct((b, s, d), BF16),
        compiler_params=_params("arbitrary", "arbitrary"),
        name="stick_breaking_attention",
    )(jnp.asarray(table), qkv, qkv, qkv, meta_qkv, meta_qkv, u, bias)


def kernel(x, meta_tokens, conv_norm, conv_w_in, conv_w, conv_w_out, attn_norm,
           attn_w_qkv, attn_w_out, mlp_norm, mlp_w1, mlp_w2, final_norm):
    b, s, d = x.shape
    depth = mlp_norm.shape[0]
    hr = x.reshape(b * s, d)
    hm = jnp.pad(meta_tokens.astype(x.dtype), ((0, META_ROWS - N_META), (0, 0)))
    zero_hist = jnp.zeros((SUBLANES, d), F32)
    rows = lambda g: g.reshape(-1, 1, d)

    w1, w2 = mlp_w1.astype(BF16), mlp_w2.astype(BF16)
    w_in, w_out = conv_w_in.astype(BF16), conv_w_out.astype(BF16)
    w_qkv, w_o = attn_w_qkv.astype(BF16), attn_w_out.astype(BF16)
    g_conv, g_attn, g_mlp = rows(conv_norm), rows(attn_norm), rows(mlp_norm)
    g_final = final_norm.reshape(1, d)

    for i in range(depth):
        j = i // 2
        last = i == depth - 1
        final_g = g_final if last else None
        if i % 2 == 0:
            hm, hist = _conv_mixer(hm, g_conv, w_in, conv_w, w_out, zero_hist, layer=j,
                                   tm=META_ROWS, tiles_per_seq=1, hist_row=N_META)
            hr = _conv_mixer(hr, g_conv, w_in, conv_w, w_out, hist, layer=j,
                             tm=ROW_TILE, tiles_per_seq=s // ROW_TILE)
            hm = _mlp(hm, g_mlp, w1, w2, layer=i, tm=META_ROWS)
            hr = _mlp(hr, g_mlp, w1, w2, layer=i, tm=ROW_TILE, final_g=final_g)
        else:
            qkv_m = _qkv(hm, g_attn, w_qkv, layer=j, tm=META_ROWS)
            qkv_r = _qkv(hr, g_attn, w_qkv, layer=j, tm=ROW_TILE)
            o_r = _attention(qkv_r.reshape(b, s, 3 * d), qkv_m, t=ATTN_TILE)
            if not last:
                o_m = _attention_small(qkv_m)
                hm = _mlp(hm, g_mlp, w1, w2, layer=i, tm=META_ROWS,
                          attn=o_m, w_o=w_o, w_o_layer=j)
            hr = _mlp(hr, g_mlp, w1, w2, layer=i, tm=ROW_TILE,
                      attn=o_r.reshape(b * s, d), w_o=w_o, w_o_layer=j, final_g=final_g)
    return hr.reshape(b, s, d)
```

```python
import functools
import math

import numpy as np
import jax
import jax.numpy as jnp
from jax import lax
from jax.experimental import pallas as pl
from jax.experimental.pallas import tpu as pltpu

D_MODEL = 1024
N_META = 16
N_HEADS = 16
HEAD_DIM = D_MODEL // N_HEADS
CONV_W = 3
D_FF = 4 * D_MODEL
RMS_EPS = 1e-6

LANES = 128
SUBLANES = 8
HEADS_PER_GROUP = LANES // HEAD_DIM
N_GROUPS = N_HEADS // HEADS_PER_GROUP
META_ROWS = LANES
ROW_TILE = 512
ATTN_TILE = 256
FF_CHUNK = 1024
VMEM_LIMIT_BYTES = 56 * 1024 * 1024
LOG2_E = math.log2(math.e)
MASK_BIAS = -1e30

BF16 = jnp.bfloat16
F32 = jnp.float32


def _dot(a, b):
    return jnp.dot(a, b, preferred_element_type=F32)


def _rmsnorm(x, g):
    return x * lax.rsqrt(jnp.mean(x * x, axis=-1, keepdims=True) + RMS_EPS) * g


def _resident(shape, layer=None):
    zeros = (0,) * len(shape)
    if layer is None:
        return pl.BlockSpec(shape, lambda *_: zeros, pipeline_mode=pl.Buffered(1))
    return pl.BlockSpec((None,) + tuple(shape), lambda *_: (layer,) + zeros,
                        pipeline_mode=pl.Buffered(1))


def _params(*semantics):
    return pltpu.CompilerParams(dimension_semantics=semantics,
                                vmem_limit_bytes=VMEM_LIMIT_BYTES)


def _conv_mixer_kernel(x_ref, g_ref, w_in_ref, cw_ref, w_out_ref, hist_ref,
                       *rest, tm, tiles_per_seq, hist_row):
    if hist_row is None:
        o_ref, ubuf = rest
    else:
        o_ref, hist_out_ref, ubuf = rest
    i = pl.program_id(0)
    first = (i % tiles_per_seq) == 0

    @pl.when(first)
    def _():
        ubuf[0:SUBLANES, :] = hist_ref[...]

    @pl.when(jnp.logical_not(first))
    def _():
        ubuf[0:SUBLANES, :] = ubuf[tm:tm + SUBLANES, :]

    x = x_ref[...]
    h = _rmsnorm(x, g_ref[...]).astype(BF16)
    d = D_MODEL
    gate_c = _dot(h, w_in_ref[:, d:2 * d])
    val = _dot(h, w_in_ref[:, 2 * d:3 * d])
    ubuf[SUBLANES:SUBLANES + tm, :] = gate_c * val
    if hist_row is not None:
        hist_out_ref[...] = ubuf[hist_row:hist_row + SUBLANES, :]
    cw = cw_ref[...]
    conv = ubuf[SUBLANES:SUBLANES + tm, :] * cw[CONV_W - 1:CONV_W, :]
    for k in range(CONV_W - 1):
        back = CONV_W - 1 - k
        conv = conv + ubuf[SUBLANES - back:SUBLANES - back + tm, :] * cw[k:k + 1, :]
    gate_b = _dot(h, w_in_ref[:, 0:d])
    y = (gate_b * conv).astype(BF16)
    o_ref[...] = x + _dot(y, w_out_ref[...])


def _conv_mixer(x, g, w_in, cw, w_out, hist, *, layer, tm, tiles_per_seq, hist_row=None):
    n, d = x.shape
    kernel = functools.partial(_conv_mixer_kernel, tm=tm,
                               tiles_per_seq=tiles_per_seq, hist_row=hist_row)
    row = pl.BlockSpec((tm, d), lambda i: (i, 0))
    out_shape = jax.ShapeDtypeStruct((n, d), F32)
    out_specs = row
    if hist_row is not None:
        out_shape = (out_shape, jax.ShapeDtypeStruct((SUBLANES, d), F32))
        out_specs = (row, pl.BlockSpec((SUBLANES, d), lambda i: (0, 0)))
    return pl.pallas_call(
        kernel,
        grid=(n // tm,),
        in_specs=[row, _resident((1, d), layer), _resident((d, 3 * d), layer),
                  _resident((CONV_W, d), layer), _resident((d, d), layer),
                  _resident((SUBLANES, d))],
        out_specs=out_specs,
        out_shape=out_shape,
        scratch_shapes=[pltpu.VMEM((tm + 2 * SUBLANES, d), F32)],
        compiler_params=_params("arbitrary"),
        name="conv_mixer",
    )(x, g, w_in, cw, w_out, hist)


def _mlp_kernel(*refs, with_proj, with_final_norm):
    refs = list(refs)
    x_ref = refs.pop(0)
    if with_proj:
        a_ref, w_o_ref = refs.pop(0), refs.pop(0)
    g_ref, w1_ref, w2_ref = refs.pop(0), refs.pop(0), refs.pop(0)
    if with_final_norm:
        gf_ref = refs.pop(0)
    (o_ref,) = refs

    x = x_ref[...]
    if with_proj:
        x = x + _dot(a_ref[...], w_o_ref[...])
    h = _rmsnorm(x, g_ref[...]).astype(BF16)
    acc = x
    for c in range(D_FF // FF_CHUNK):
        cols = slice(c * FF_CHUNK, (c + 1) * FF_CHUNK)
        a = jnp.maximum(_dot(h, w1_ref[:, cols]), 0.0)
        acc = acc + _dot((a * a).astype(BF16), w2_ref[cols, :])
    if with_final_norm:
        acc = _rmsnorm(acc, gf_ref[...])
    o_ref[...] = acc


def _mlp(x, g, w1, w2, *, layer, tm, attn=None, w_o=None, w_o_layer=None, final_g=None):
    n, d = x.shape
    with_proj = attn is not None
    with_final_norm = final_g is not None
    row = pl.BlockSpec((tm, d), lambda i: (i, 0))
    args, specs = [x], [row]
    if with_proj:
        args += [attn, w_o]
        specs += [row, _resident((d, d), w_o_layer)]
    args += [g, w1, w2]
    specs += [_resident((1, d), layer), _resident((d, D_FF), layer),
              _resident((D_FF, d), layer)]
    if with_final_norm:
        args.append(final_g)
        specs.append(_resident((1, d)))
    kernel = functools.partial(_mlp_kernel, with_proj=with_proj,
                               with_final_norm=with_final_norm)
    return pl.pallas_call(
        kernel,
        grid=(n // tm,),
        in_specs=specs,
        out_specs=row,
        out_shape=jax.ShapeDtypeStruct((n, d), F32),
        compiler_params=_params("arbitrary"),
        name="mlp",
    )(*args)


def _qkv_kernel(x_ref, g_ref, w_qkv_ref, qkv_ref):
    d = D_MODEL
    h = _rmsnorm(x_ref[...], g_ref[...]).astype(BF16)
    scale = 1.0 / math.sqrt(HEAD_DIM)
    qkv_ref[:, 0:d] = (_dot(h, w_qkv_ref[:, 0:d]) * scale).astype(BF16)
    qkv_ref[:, d:3 * d] = _dot(h, w_qkv_ref[:, d:3 * d]).astype(BF16)


def _qkv(x, g, w_qkv, *, layer, tm):
    n, d = x.shape
    return pl.pallas_call(
        _qkv_kernel,
        grid=(n // tm,),
        in_specs=[pl.BlockSpec((tm, d), lambda i: (i, 0)), _resident((1, d), layer),
                  _resident((d, 3 * d), layer)],
        out_specs=pl.BlockSpec((tm, 3 * d), lambda i: (i, 0)),
        out_shape=jax.ShapeDtypeStruct((n, 3 * d), BF16),
        compiler_params=_params("arbitrary"),
        name="qkv_proj",
    )(x, g, w_qkv)


def _suffix_sum_matrix(t, sign):
    r = jnp.arange(t)[:, None]
    c = jnp.arange(t + LANES)[None, :]
    return jnp.where((c >= t) | (r > c), sign, 0.0).astype(BF16)


def _head_of_lane(shape):
    return lax.broadcasted_iota(jnp.int32, shape, len(shape) - 1) // HEAD_DIM


def _attn_small_kernel(q_ref, k_ref, v_ref, u_ref, o_ref, *, t):
    q = q_ref[...]
    k = k_ref[...]
    head_of_lane = _head_of_lane((t, LANES))
    row_i = lax.broadcasted_iota(jnp.int32, (t, t), 0)
    col_i = lax.broadcasted_iota(jnp.int32, (t, t), 1)
    causal = col_i < row_i
    out = None
    for hh in range(HEADS_PER_GROUP):
        qh = jnp.where(head_of_lane == hh, q, jnp.zeros_like(q))
        z = lax.dot_general(qh, k, (((1,), (1,)), ((), ())), preferred_element_type=F32)
        soft = jnp.log(1.0 + jnp.exp(-jnp.abs(z)))
        log_beta = jnp.minimum(z, 0.0) - soft
        log_1m = jnp.where(causal, log_beta - z, 0.0)
        tail = _dot(log_1m.astype(BF16), u_ref[:, 0:t])
        w = jnp.where(causal, jnp.exp(log_beta + tail), 0.0)
        oh = _dot(w.astype(BF16), v_ref[...])
        out = oh if out is None else jnp.where(head_of_lane == hh, oh, out)
    o_ref[...] = out.astype(BF16)


def _attention_small(qkv):
    t = qkv.shape[0]
    d = D_MODEL
    u = _suffix_sum_matrix(t, 1.0)
    col0 = d // LANES
    return pl.pallas_call(
        functools.partial(_attn_small_kernel, t=t),
        grid=(N_GROUPS,),
        in_specs=[pl.BlockSpec((t, LANES), lambda g: (0, g)),
                  pl.BlockSpec((t, LANES), lambda g: (0, col0 + g)),
                  pl.BlockSpec((t, LANES), lambda g: (0, 2 * col0 + g)),
                  _resident(u.shape)],
        out_specs=pl.BlockSpec((t, LANES), lambda g: (0, g)),
        out_shape=jax.ShapeDtypeStruct((t, d), BF16),
        compiler_params=_params("arbitrary"),
        name="attention_meta",
    )(qkv, qkv, qkv, u)


_KIND_PLAIN, _KIND_DIAG, _KIND_META = 0, 1, 2
_PIPE_DEPTH = 4
_UNROLL = 34
_ITEM_FIELDS = 6
_PIPELINED_BLOCKS = 2
assert _PIPELINED_BLOCKS == 2 and _PIPE_DEPTH % 2 == 0 and _UNROLL % 2 == 0
STEP_GROUPS = 4
LOG_UNDERFLOW = -105.0


def _tile_blocks(qi, s, t):
    blocks = [(qi * t, _KIND_DIAG)]
    blocks += [(j * t, _KIND_PLAIN) for j in range(qi - 1, -1, -1)]
    blocks.append((s, _KIND_META))
    return blocks


def _attention_items(s, t):
    n_tiles = s // t
    items = []
    for grp in range(STEP_GROUPS):
        for qi in range(n_tiles):
            for k_off, kind in _tile_blocks(qi, s, t)[:_PIPELINED_BLOCKS]:
                items.append([qi * t, k_off, kind, grp * n_tiles + qi, grp, qi * t])
    lead = _PIPE_DEPTH - 1
    n_iter = len(items) + lead
    n_iter += -n_iter % _UNROLL
    first_pair, last_pair = items[:2], items[-2:]
    table = (first_pair * 2)[-lead:] + items
    while len(table) < n_iter + lead:
        table.append(last_pair[(len(table) - lead) % 2])
    return np.asarray(table, np.int32).reshape(-1), n_iter


def _mask_bias(t):
    r = np.arange(t)[:, None]
    c = np.arange(t)[None, :]
    bias = np.zeros((3, t, t), np.float32)
    bias[_KIND_DIAG] = np.where(c < r, 0.0, MASK_BIAS)
    bias[_KIND_META] = np.where(c < N_META, 0.0, MASK_BIAS) + 0.0 * r
    return jnp.asarray(bias)


def _attn_pipe_kernel(tab_ref, q_ref, k_ref, v_ref, km_ref, vm_ref, u_ref, bias_ref,
                      o_ref, kt_all, v_all, zbuf, lbuf, pbuf, wbuf, carry_ref,
                      acc_ref, carry_all, acc_all, open_ref, *, s, t, n_iter):
    nh = HEADS_PER_GROUP
    first_step = jnp.logical_and(pl.program_id(0) == 0, pl.program_id(1) == 0)

    @pl.when(first_step)
    def _():
        for ref in (zbuf, lbuf, pbuf, wbuf, carry_ref, acc_ref):
            ref[...] = jnp.zeros_like(ref)
        kt_all[...] = jnp.zeros_like(kt_all)
        v_all[:, s + META_ROWS:s + t, :] = jnp.zeros(
            (STEP_GROUPS * nh, t - META_ROWS, LANES), BF16)

    head_of_lane = _head_of_lane((1, LANES))
    for grp in range(STEP_GROUPS):
        cols = slice(grp * LANES, (grp + 1) * LANES)
        v = v_ref[:, cols]
        vm = vm_ref[:, cols]
        for hh in range(nh):
            mine = head_of_lane == hh
            v_all[grp * nh + hh, 0:s, :] = jnp.where(mine, v, jnp.zeros_like(v))
            v_all[grp * nh + hh, s:s + META_ROWS, :] = jnp.where(mine, vm, jnp.zeros_like(vm))
        chunks = [(k_ref, c, c) for c in range(0, s, t)] + [(km_ref, 0, s)]
        for src_ref, src_row, dst_col in chunks:
            n_keys = min(t, src_ref.shape[0])
            kt = src_ref[src_row:src_row + n_keys, cols].T
            for hh in range(nh):
                rows = slice(hh * HEAD_DIM, (hh + 1) * HEAD_DIM)
                kt_all[grp * nh + hh, rows, dst_col:dst_col + n_keys] = kt[rows, :]

    def stage_scores(q_off, k_off, grp, p):
        q_blk = q_ref[pl.ds(pl.multiple_of(q_off, t), t),
                      pl.ds(pl.multiple_of(grp * LANES, LANES), LANES)]
        for hh in range(nh):
            zbuf[p, hh] = _dot(q_blk, kt_all[grp * nh + hh, :, pl.ds(pl.multiple_of(k_off, t), t)])

    def stage_logs(kind, p):
        bias = bias_ref[kind]
        for hh in range(nh):
            z = zbuf[1 - p, hh] + bias
            soft = jnp.log(1.0 + jnp.exp2(jnp.abs(z) * (-LOG2_E)))
            neg_log_1m = jnp.maximum(z, 0.0) + soft
            lbuf[p, hh] = z - neg_log_1m
            pbuf[p, hh] = neg_log_1m.astype(BF16)

    def stage_weights(first, p):
        for hh in range(nh):
            sums = _dot(pbuf[1 - p, hh], u_ref[...])
            tail, row_sum = sums[:, 0:t], sums[:, t:t + LANES]
            if first:
                carry_ref[hh] = row_sum
            else:
                carry = carry_ref[hh]
                tail = tail + jnp.concatenate([carry] * (t // LANES), axis=1)
                carry_ref[hh] = carry + row_sum
            wbuf[p, hh] = jnp.exp(lbuf[1 - p, hh] + tail).astype(BF16)

    def stage_output(q_off, k_off, grp, first, p):
        k_off = pl.multiple_of(k_off, t)
        w_cat = jnp.concatenate([wbuf[1 - p, hh] for hh in range(nh)], axis=1)
        v_cat = jnp.concatenate([v_all[grp * nh + hh, pl.ds(k_off, t), :]
                                 for hh in range(nh)], axis=0)
        acc = _dot(w_cat, v_cat)
        if not first:
            acc = acc_ref[...] + acc
            o_ref[pl.ds(pl.multiple_of(q_off, t), t),
                  pl.ds(pl.multiple_of(grp * LANES, LANES), LANES)] = acc.astype(BF16)
        acc_ref[...] = acc

    def field(item, f):
        return tab_ref[item * _ITEM_FIELDS + f]

    def iteration(i, parity):
        p = parity
        stage_logs(field(i + 2, 2), p)
        if parity == 0:
            stage_weights(True, p)
        else:
            stage_weights(False, p)
            carry_all[field(i + 1, 3)] = carry_ref[...]
        stage_scores(field(i + 3, 5), field(i + 3, 1), field(i + 3, 4), p)
        stage_output(field(i, 0), field(i, 1), field(i, 4), parity == 1, p)
        if parity == 0:
            acc_all[field(i, 3)] = acc_ref[...]

    def body(k, c):
        for j in range(_UNROLL):
            iteration(_UNROLL * k + j, j % 2)
        return c

    lax.fori_loop(0, n_iter // _UNROLL, body, 0)

    def saturated():
        return jnp.max(carry_ref[...]) <= LOG_UNDERFLOW

    n_tiles = s // t
    first_open = _PIPELINED_BLOCKS - 1

    def finish_tile(slot, c):
        grp = slot // n_tiles
        qi = slot % n_tiles
        n_blocks = qi + 2

        @pl.when(open_ref[slot] != 0)
        def _():
            carry_ref[...] = carry_all[slot]
            acc_ref[...] = acc_all[slot]
            q_off = qi * t

            def more(state):
                n, done = state
                return jnp.logical_and(n < n_blocks, jnp.logical_not(done))

            def one_block(state):
                n, _ = state
                is_meta = n == n_blocks - 1
                k_off = jnp.where(is_meta, s, (qi - n) * t)
                kind = jnp.where(is_meta, _KIND_META, _KIND_PLAIN)
                stage_scores(q_off, k_off, grp, 0)
                stage_logs(kind, 1)
                stage_weights(None, 0)
                stage_output(q_off, k_off, grp, None, 1)
                return n + 1, saturated()

            lax.while_loop(more, one_block, (jnp.int32(_PIPELINED_BLOCKS), jnp.bool_(False)))

        return c

    open_slots = [grp * n_tiles + qi for grp in range(STEP_GROUPS)
                  for qi in range(first_open, n_tiles)]
    worst = carry_all[open_slots[0]]
    for slot in open_slots[1:]:
        worst = jnp.maximum(worst, carry_all[slot])

    @pl.when(jnp.max(worst) > LOG_UNDERFLOW)
    def _():
        for slot in range(STEP_GROUPS * n_tiles):
            if slot in open_slots:
                open_ref[slot] = (jnp.max(carry_all[slot]) > LOG_UNDERFLOW).astype(jnp.int32)
            else:
                open_ref[slot] = 0
        lax.fori_loop(0, STEP_GROUPS * n_tiles, finish_tile, 0)


def _attention(qkv, meta_qkv, *, t):
    b, s, _ = qkv.shape
    d = D_MODEL
    nh = HEADS_PER_GROUP
    table, n_iter = _attention_items(s, t)
    u = _suffix_sum_matrix(t, -1.0)
    bias = _mask_bias(t)
    width = STEP_GROUPS * LANES
    col0 = d // width
    n_slots = STEP_GROUPS * (s // t)
    grid_spec = pltpu.PrefetchScalarGridSpec(
        num_scalar_prefetch=1,
        grid=(b, N_GROUPS // STEP_GROUPS),
        in_specs=[
            pl.BlockSpec((None, s, width), lambda bi, g, tab: (bi, 0, g)),
            pl.BlockSpec((None, s, width), lambda bi, g, tab: (bi, 0, col0 + g)),
            pl.BlockSpec((None, s, width), lambda bi, g, tab: (bi, 0, 2 * col0 + g)),
            pl.BlockSpec((META_ROWS, width), lambda bi, g, tab: (0, col0 + g)),
            pl.BlockSpec((META_ROWS, width), lambda bi, g, tab: (0, 2 * col0 + g)),
            pl.BlockSpec(u.shape, lambda bi, g, tab: (0, 0)),
            pl.BlockSpec(bias.shape, lambda bi, g, tab: (0, 0, 0)),
        ],
        out_specs=pl.BlockSpec((None, s, width), lambda bi, g, tab: (bi, 0, g)),
        scratch_shapes=[
            pltpu.VMEM((STEP_GROUPS * nh, LANES, s + t), BF16),
            pltpu.VMEM((STEP_GROUPS * nh, s + t, LANES), BF16),
            pltpu.VMEM((2, nh, t, t), F32),
            pltpu.VMEM((2, nh, t, t), F32),
            pltpu.VMEM((2, nh, t, t), BF16),
            pltpu.VMEM((2, nh, t, t), BF16),
            pltpu.VMEM((nh, t, LANES), F32),
            pltpu.VMEM((t, LANES), F32),
            pltpu.VMEM((n_slots, nh, t, LANES), F32),
            pltpu.VMEM((n_slots, t, LANES), F32),
            pltpu.SMEM((n_slots,), jnp.int32),
        ],
    )
    kernel = functools.partial(_attn_pipe_kernel, s=s, t=t, n_iter=n_iter)
    return pl.pallas_call(
        kernel,
        grid_spec=grid_spec,
        out_shape=jax.ShapeDtypeStruct((b, s, d), BF16),
        compiler_params=_params("arbitrary", "arbitrary"),
        name="stick_breaking_attention",
    )(jnp.asarray(table), qkv, qkv, qkv, meta_qkv, meta_qkv, u, bias)


def kernel(x, meta_tokens, conv_norm, conv_w_in, conv_w, conv_w_out, attn_norm,
           attn_w_qkv, attn_w_out, mlp_norm, mlp_w1, mlp_w2, final_norm):
    b, s, d = x.shape
    depth = mlp_norm.shape[0]
    hr = x.reshape(b * s, d)
    hm = jnp.pad(meta_tokens.astype(x.dtype), ((0, META_ROWS - N_META), (0, 0)))
    zero_hist = jnp.zeros((SUBLANES, d), F32)
    rows = lambda g: g.reshape(-1, 1, d)

    w1, w2 = mlp_w1.astype(BF16), mlp_w2.astype(BF16)
    w_in, w_out = conv_w_in.astype(BF16), conv_w_out.astype(BF16)
    w_qkv, w_o = attn_w_qkv.astype(BF16), attn_w_out.astype(BF16)
    g_conv, g_attn, g_mlp = rows(conv_norm), rows(attn_norm), rows(mlp_norm)
    g_final = final_norm.reshape(1, d)

    for i in range(depth):
        j = i // 2
        last = i == depth - 1
        final_g = g_final if last else None
        if i % 2 == 0:
            hm, hist = _conv_mixer(hm, g_conv, w_in, conv_w, w_out, zero_hist, layer=j,
                                   tm=META_ROWS, tiles_per_seq=1, hist_row=N_META)
            hr = _conv_mixer(hr, g_conv, w_in, conv_w, w_out, hist, layer=j,
                             tm=ROW_TILE, tiles_per_seq=s // ROW_TILE)
            hm = _mlp(hm, g_mlp, w1, w2, layer=i, tm=META_ROWS)
            hr = _mlp(hr, g_mlp, w1, w2, layer=i, tm=ROW_TILE, final_g=final_g)
        else:
            qkv_m = _qkv(hm, g_attn, w_qkv, layer=j, tm=META_ROWS)
            qkv_r = _qkv(hr, g_attn, w_qkv, layer=j, tm=ROW_TILE)
            o_r = _attention(qkv_r.reshape(b, s, 3 * d), qkv_m, t=ATTN_TILE)
            if not last:
                o_m = _attention_small(qkv_m)
                hm = _mlp(hm, g_mlp, w1, w2, layer=i, tm=META_ROWS,
                          attn=o_m, w_o=w_o, w_o_layer=j)
            hr = _mlp(hr, g_mlp, w1, w2, layer=i, tm=ROW_TILE,
                      attn=o_r.reshape(b * s, d), w_o=w_o, w_o_layer=j, final_g=final_g)
    return hr.reshape(b, s, d)
```

```python
import functools
import math

import numpy as np
import jax
import jax.numpy as jnp
from jax import lax
from jax.experimental import pallas as pl
from jax.experimental.pallas import tpu as pltpu

D_MODEL = 1024
N_META = 16
N_HEADS = 16
HEAD_DIM = D_MODEL // N_HEADS
CONV_W = 3
D_FF = 4 * D_MODEL
RMS_EPS = 1e-6

LANES = 128
SUBLANES = 8
HEADS_PER_GROUP = LANES // HEAD_DIM
N_GROUPS = N_HEADS // HEADS_PER_GROUP
META_ROWS = LANES
ROW_TILE = 512
ATTN_TILE = 256
FF_CHUNK = 1024
VMEM_LIMIT_BYTES = 56 * 1024 * 1024
LOG2_E = math.log2(math.e)
MASK_BIAS = -1e30

BF16 = jnp.bfloat16
F32 = jnp.float32


def _dot(a, b):
    return jnp.dot(a, b, preferred_element_type=F32)


def _rmsnorm(x, g):
    return x * lax.rsqrt(jnp.mean(x * x, axis=-1, keepdims=True) + RMS_EPS) * g


def _resident(shape):
    zeros = (0,) * len(shape)
    return pl.BlockSpec(shape, lambda *_: zeros, pipeline_mode=pl.Buffered(1))


def _cast_specs(casts, n_steps):
    in_specs, out_specs, out_shapes = [], [], []
    for stack, layer in casts:
        _, rows, cols = stack.shape
        chunk = rows // n_steps
        assert chunk * n_steps == rows and chunk % (2 * SUBLANES) == 0, (rows, n_steps)
        in_specs.append(pl.BlockSpec((None, chunk, cols), lambda i, layer=layer: (layer, i, 0)))
        out_specs.append(pl.BlockSpec((chunk, cols), lambda i: (i, 0)))
        out_shapes.append(jax.ShapeDtypeStruct((rows, cols), BF16))
    return in_specs, out_specs, out_shapes


def _do_casts(src_refs, dst_refs):
    for src, dst in zip(src_refs, dst_refs, strict=True):
        dst[...] = src[...].astype(BF16)


def _params(*semantics):
    return pltpu.CompilerParams(dimension_semantics=semantics,
                                vmem_limit_bytes=VMEM_LIMIT_BYTES)


def _conv_mixer_kernel(x_ref, g_ref, w_in_ref, cw_ref, w_out_ref, hist_ref,
                       *rest, tm, tiles_per_seq, hist_row, n_casts):
    rest = list(rest)
    cast_src = [rest.pop(0) for _ in range(n_casts)]
    o_ref = rest.pop(0)
    hist_out_ref = None if hist_row is None else rest.pop(0)
    cast_dst = [rest.pop(0) for _ in range(n_casts)]
    (ubuf,) = rest
    _do_casts(cast_src, cast_dst)
    i = pl.program_id(0)
    first = (i % tiles_per_seq) == 0

    @pl.when(first)
    def _():
        ubuf[0:SUBLANES, :] = hist_ref[...]

    @pl.when(jnp.logical_not(first))
    def _():
        ubuf[0:SUBLANES, :] = ubuf[tm:tm + SUBLANES, :]

    x = x_ref[...]
    h = _rmsnorm(x, g_ref[...]).astype(BF16)
    d = D_MODEL
    gate_c = _dot(h, w_in_ref[:, d:2 * d])
    val = _dot(h, w_in_ref[:, 2 * d:3 * d])
    ubuf[SUBLANES:SUBLANES + tm, :] = gate_c * val
    if hist_row is not None:
        hist_out_ref[...] = ubuf[hist_row:hist_row + SUBLANES, :]
    cw = cw_ref[...]
    conv = ubuf[SUBLANES:SUBLANES + tm, :] * cw[CONV_W - 1:CONV_W, :]
    for k in range(CONV_W - 1):
        back = CONV_W - 1 - k
        conv = conv + ubuf[SUBLANES - back:SUBLANES - back + tm, :] * cw[k:k + 1, :]
    gate_b = _dot(h, w_in_ref[:, 0:d])
    y = (gate_b * conv).astype(BF16)
    o_ref[...] = x + _dot(y, w_out_ref[...])


def _conv_mixer(x, g, w_in, cw, w_out, hist, *, tm, tiles_per_seq, hist_row=None, casts=()):
    n, d = x.shape
    n_steps = n // tm
    kernel = functools.partial(_conv_mixer_kernel, tm=tm, tiles_per_seq=tiles_per_seq,
                               hist_row=hist_row, n_casts=len(casts))
    row = pl.BlockSpec((tm, d), lambda i: (i, 0))
    cast_in, cast_out, cast_shapes = _cast_specs(casts, n_steps)
    out_shapes = [jax.ShapeDtypeStruct((n, d), F32)]
    out_specs = [row]
    if hist_row is not None:
        out_shapes.append(jax.ShapeDtypeStruct((SUBLANES, d), F32))
        out_specs.append(pl.BlockSpec((SUBLANES, d), lambda i: (0, 0)))
    return pl.pallas_call(
        kernel,
        grid=(n_steps,),
        in_specs=[row, _resident((1, d)), _resident((d, 3 * d)), _resident((CONV_W, d)),
                  _resident((d, d)), _resident((SUBLANES, d))] + cast_in,
        out_specs=out_specs + cast_out,
        out_shape=out_shapes + cast_shapes,
        scratch_shapes=[pltpu.VMEM((tm + 2 * SUBLANES, d), F32)],
        compiler_params=_params("arbitrary"),
        name="conv_mixer",
    )(x, g, w_in, cw, w_out, hist, *[stack for stack, _ in casts])


def _mlp_kernel(*refs, with_proj, with_final_norm, n_casts):
    refs = list(refs)
    x_ref = refs.pop(0)
    if with_proj:
        a_ref, w_o_ref = refs.pop(0), refs.pop(0)
    g_ref, w1_ref, w2_ref = refs.pop(0), refs.pop(0), refs.pop(0)
    if with_final_norm:
        gf_ref = refs.pop(0)
    cast_src = [refs.pop(0) for _ in range(n_casts)]
    o_ref = refs.pop(0)
    _do_casts(cast_src, refs)

    x = x_ref[...]
    if with_proj:
        x = x + _dot(a_ref[...], w_o_ref[...])
    h = _rmsnorm(x, g_ref[...]).astype(BF16)
    acc = x
    for c in range(D_FF // FF_CHUNK):
        cols = slice(c * FF_CHUNK, (c + 1) * FF_CHUNK)
        a = jnp.maximum(_dot(h, w1_ref[:, cols]), 0.0)
        acc = acc + _dot((a * a).astype(BF16), w2_ref[cols, :])
    if with_final_norm:
        acc = _rmsnorm(acc, gf_ref[...])
    o_ref[...] = acc


def _mlp(x, g, w1, w2, *, tm, attn=None, w_o=None, final_g=None, casts=()):
    n, d = x.shape
    n_steps = n // tm
    with_proj = attn is not None
    with_final_norm = final_g is not None
    row = pl.BlockSpec((tm, d), lambda i: (i, 0))
    args, specs = [x], [row]
    if with_proj:
        args += [attn, w_o]
        specs += [row, _resident((d, d))]
    args += [g, w1, w2]
    specs += [_resident((1, d)), _resident((d, D_FF)), _resident((D_FF, d))]
    if with_final_norm:
        args.append(final_g)
        specs.append(_resident((1, d)))
    cast_in, cast_out, cast_shapes = _cast_specs(casts, n_steps)
    kernel = functools.partial(_mlp_kernel, with_proj=with_proj,
                               with_final_norm=with_final_norm, n_casts=len(casts))
    return pl.pallas_call(
        kernel,
        grid=(n_steps,),
        in_specs=specs + cast_in,
        out_specs=[row] + cast_out,
        out_shape=[jax.ShapeDtypeStruct((n, d), F32)] + cast_shapes,
        compiler_params=_params("arbitrary"),
        name="mlp",
    )(*args, *[stack for stack, _ in casts])


def _qkv_kernel(x_ref, g_ref, w_qkv_ref, qkv_ref):
    d = D_MODEL
    h = _rmsnorm(x_ref[...], g_ref[...]).astype(BF16)
    scale = 1.0 / math.sqrt(HEAD_DIM)
    qkv_ref[:, 0:d] = (_dot(h, w_qkv_ref[:, 0:d]) * scale).astype(BF16)
    qkv_ref[:, d:3 * d] = _dot(h, w_qkv_ref[:, d:3 * d]).astype(BF16)


def _qkv(x, g, w_qkv, *, tm):
    n, d = x.shape
    return pl.pallas_call(
        _qkv_kernel,
        grid=(n // tm,),
        in_specs=[pl.BlockSpec((tm, d), lambda i: (i, 0)), _resident((1, d)),
                  _resident((d, 3 * d))],
        out_specs=pl.BlockSpec((tm, 3 * d), lambda i: (i, 0)),
        out_shape=jax.ShapeDtypeStruct((n, 3 * d), BF16),
        compiler_params=_params("arbitrary"),
        name="qkv_proj",
    )(x, g, w_qkv)


def _suffix_sum_matrix(t, sign):
    r = jnp.arange(t)[:, None]
    c = jnp.arange(t + LANES)[None, :]
    return jnp.where((c >= t) | (r > c), sign, 0.0).astype(BF16)


def _head_of_lane(shape):
    return lax.broadcasted_iota(jnp.int32, shape, len(shape) - 1) // HEAD_DIM


def _attn_small_kernel(q_ref, k_ref, v_ref, u_ref, o_ref, *, t):
    q = q_ref[...]
    k = k_ref[...]
    head_of_lane = _head_of_lane((t, LANES))
    row_i = lax.broadcasted_iota(jnp.int32, (t, t), 0)
    col_i = lax.broadcasted_iota(jnp.int32, (t, t), 1)
    causal = col_i < row_i
    out = None
    for hh in range(HEADS_PER_GROUP):
        qh = jnp.where(head_of_lane == hh, q, jnp.zeros_like(q))
        z = lax.dot_general(qh, k, (((1,), (1,)), ((), ())), preferred_element_type=F32)
        soft = jnp.log(1.0 + jnp.exp(-jnp.abs(z)))
        log_beta = jnp.minimum(z, 0.0) - soft
        log_1m = jnp.where(causal, log_beta - z, 0.0)
        tail = _dot(log_1m.astype(BF16), u_ref[:, 0:t])
        w = jnp.where(causal, jnp.exp(log_beta + tail), 0.0)
        oh = _dot(w.astype(BF16), v_ref[...])
        out = oh if out is None else jnp.where(head_of_lane == hh, oh, out)
    o_ref[...] = out.astype(BF16)


def _attention_small(qkv):
    t = qkv.shape[0]
    d = D_MODEL
    u = _suffix_sum_matrix(t, 1.0)
    col0 = d // LANES
    return pl.pallas_call(
        functools.partial(_attn_small_kernel, t=t),
        grid=(N_GROUPS,),
        in_specs=[pl.BlockSpec((t, LANES), lambda g: (0, g)),
                  pl.BlockSpec((t, LANES), lambda g: (0, col0 + g)),
                  pl.BlockSpec((t, LANES), lambda g: (0, 2 * col0 + g)),
                  _resident(u.shape)],
        out_specs=pl.BlockSpec((t, LANES), lambda g: (0, g)),
        out_shape=jax.ShapeDtypeStruct((t, d), BF16),
        compiler_params=_params("arbitrary"),
        name="attention_meta",
    )(qkv, qkv, qkv, u)


_KIND_PLAIN, _KIND_DIAG, _KIND_META = 0, 1, 2
_PIPE_DEPTH = 4
_UNROLL = 34
_ITEM_FIELDS = 6
_PIPELINED_BLOCKS = 2
assert _PIPELINED_BLOCKS == 2 and _PIPE_DEPTH % 2 == 0 and _UNROLL % 2 == 0
STEP_GROUPS = 4
LOG_UNDERFLOW = -105.0


def _tile_blocks(qi, s, t):
    blocks = [(qi * t, _KIND_DIAG)]
    blocks += [(j * t, _KIND_PLAIN) for j in range(qi - 1, -1, -1)]
    blocks.append((s, _KIND_META))
    return blocks


def _attention_items(s, t):
    n_tiles = s // t
    items = []
    for grp in range(STEP_GROUPS):
        for qi in range(n_tiles):
            for k_off, kind in _tile_blocks(qi, s, t)[:_PIPELINED_BLOCKS]:
                items.append([qi * t, k_off, kind, grp * n_tiles + qi, grp, qi * t])
    lead = _PIPE_DEPTH - 1
    n_iter = len(items) + lead
    n_iter += -n_iter % _UNROLL
    first_pair, last_pair = items[:2], items[-2:]
    table = (first_pair * 2)[-lead:] + items
    while len(table) < n_iter + lead:
        table.append(last_pair[(len(table) - lead) % 2])
    return np.asarray(table, np.int32).reshape(-1), n_iter


def _mask_bias(t):
    r = np.arange(t)[:, None]
    c = np.arange(t)[None, :]
    bias = np.zeros((3, t, t), np.float32)
    bias[_KIND_DIAG] = np.where(c < r, 0.0, MASK_BIAS)
    bias[_KIND_META] = np.where(c < N_META, 0.0, MASK_BIAS) + 0.0 * r
    return jnp.asarray(bias)


def _attn_pipe_kernel(tab_ref, q_ref, k_ref, v_ref, km_ref, vm_ref, u_ref, bias_ref,
                      o_ref, kt_all, v_all, zbuf, lbuf, pbuf, wbuf, carry_ref,
                      acc_ref, carry_all, acc_all, open_ref, *, s, t, n_iter):
    nh = HEADS_PER_GROUP
    first_step = jnp.logical_and(pl.program_id(0) == 0, pl.program_id(1) == 0)

    @pl.when(first_step)
    def _():
        for ref in (zbuf, lbuf, pbuf, wbuf, carry_ref, acc_ref):
            ref[...] = jnp.zeros_like(ref)
        kt_all[...] = jnp.zeros_like(kt_all)
        v_all[:, s + META_ROWS:s + t, :] = jnp.zeros(
            (STEP_GROUPS * nh, t - META_ROWS, LANES), BF16)

    head_of_lane = _head_of_lane((1, LANES))
    for grp in range(STEP_GROUPS):
        cols = slice(grp * LANES, (grp + 1) * LANES)
        v = v_ref[:, cols]
        vm = vm_ref[:, cols]
        for hh in range(nh):
            mine = head_of_lane == hh
            v_all[grp * nh + hh, 0:s, :] = jnp.where(mine, v, jnp.zeros_like(v))
            v_all[grp * nh + hh, s:s + META_ROWS, :] = jnp.where(mine, vm, jnp.zeros_like(vm))
        chunks = [(k_ref, c, c) for c in range(0, s, t)] + [(km_ref, 0, s)]
        for src_ref, src_row, dst_col in chunks:
            n_keys = min(t, src_ref.shape[0])
            kt = src_ref[src_row:src_row + n_keys, cols].T
            for hh in range(nh):
                rows = slice(hh * HEAD_DIM, (hh + 1) * HEAD_DIM)
                kt_all[grp * nh + hh, rows, dst_col:dst_col + n_keys] = kt[rows, :]

    def stage_scores(q_off, k_off, grp, p):
        q_blk = q_ref[pl.ds(pl.multiple_of(q_off, t), t),
                      pl.ds(pl.multiple_of(grp * LANES, LANES), LANES)]
        for hh in range(nh):
            zbuf[p, hh] = _dot(q_blk, kt_all[grp * nh + hh, :, pl.ds(pl.multiple_of(k_off, t), t)])

    def stage_logs(kind, p):
        bias = bias_ref[kind]
        for hh in range(nh):
            z = zbuf[1 - p, hh] + bias
            soft = jnp.log(1.0 + jnp.exp2(jnp.abs(z) * (-LOG2_E)))
            neg_log_1m = jnp.maximum(z, 0.0) + soft
            lbuf[p, hh] = z - neg_log_1m
            pbuf[p, hh] = neg_log_1m.astype(BF16)

    def stage_weights(first, p):
        for hh in range(nh):
            sums = _dot(pbuf[1 - p, hh], u_ref[...])
            tail, row_sum = sums[:, 0:t], sums[:, t:t + LANES]
            if first:
                carry_ref[hh] = row_sum
            else:
                carry = carry_ref[hh]
                tail = tail + jnp.concatenate([carry] * (t // LANES), axis=1)
                carry_ref[hh] = carry + row_sum
            wbuf[p, hh] = jnp.exp(lbuf[1 - p, hh] + tail).astype(BF16)

    def stage_output(q_off, k_off, grp, first, p):
        k_off = pl.multiple_of(k_off, t)
        w_cat = jnp.concatenate([wbuf[1 - p, hh] for hh in range(nh)], axis=1)
        v_cat = jnp.concatenate([v_all[grp * nh + hh, pl.ds(k_off, t), :]
                                 for hh in range(nh)], axis=0)
        acc = _dot(w_cat, v_cat)
        if not first:
            acc = acc_ref[...] + acc
            o_ref[pl.ds(pl.multiple_of(q_off, t), t),
                  pl.ds(pl.multiple_of(grp * LANES, LANES), LANES)] = acc.astype(BF16)
        acc_ref[...] = acc

    def field(item, f):
        return tab_ref[item * _ITEM_FIELDS + f]

    def iteration(i, parity):
        p = parity
        stage_logs(field(i + 2, 2), p)
        if parity == 0:
            stage_weights(True, p)
        else:
            stage_weights(False, p)
            carry_all[field(i + 1, 3)] = carry_ref[...]
        stage_scores(field(i + 3, 5), field(i + 3, 1), field(i + 3, 4), p)
        stage_output(field(i, 0), field(i, 1), field(i, 4), parity == 1, p)
        if parity == 0:
            acc_all[field(i, 3)] = acc_ref[...]

    def body(k, c):
        for j in range(_UNROLL):
            iteration(_UNROLL * k + j, j % 2)
        return c

    lax.fori_loop(0, n_iter // _UNROLL, body, 0)

    def saturated():
        return jnp.max(carry_ref[...]) <= LOG_UNDERFLOW

    n_tiles = s // t
    first_open = _PIPELINED_BLOCKS - 1

    def finish_tile(slot, c):
        grp = slot // n_tiles
        qi = slot % n_tiles
        n_blocks = qi + 2

        @pl.when(open_ref[slot] != 0)
        def _():
            carry_ref[...] = carry_all[slot]
            acc_ref[...] = acc_all[slot]
            q_off = qi * t

            def more(state):
                n, done = state
                return jnp.logical_and(n < n_blocks, jnp.logical_not(done))

            def one_block(state):
                n, _ = state
                is_meta = n == n_blocks - 1
                k_off = jnp.where(is_meta, s, (qi - n) * t)
                kind = jnp.where(is_meta, _KIND_META, _KIND_PLAIN)
                stage_scores(q_off, k_off, grp, 0)
                stage_logs(kind, 1)
                stage_weights(None, 0)
                stage_output(q_off, k_off, grp, None, 1)
                return n + 1, saturated()

            lax.while_loop(more, one_block, (jnp.int32(_PIPELINED_BLOCKS), jnp.bool_(False)))

        return c

    open_slots = [grp * n_tiles + qi for grp in range(STEP_GROUPS)
                  for qi in range(first_open, n_tiles)]
    worst = carry_all[open_slots[0]]
    for slot in open_slots[1:]:
        worst = jnp.maximum(worst, carry_all[slot])

    @pl.when(jnp.max(worst) > LOG_UNDERFLOW)
    def _():
        for slot in range(STEP_GROUPS * n_tiles):
            if slot in open_slots:
                open_ref[slot] = (jnp.max(carry_all[slot]) > LOG_UNDERFLOW).astype(jnp.int32)
            else:
                open_ref[slot] = 0
        lax.fori_loop(0, STEP_GROUPS * n_tiles, finish_tile, 0)


def _attention(qkv, meta_qkv, *, t):
    b, s, _ = qkv.shape
    d = D_MODEL
    nh = HEADS_PER_GROUP
    table, n_iter = _attention_items(s, t)
    u = _suffix_sum_matrix(t, -1.0)
    bias = _mask_bias(t)
    width = STEP_GROUPS * LANES
    col0 = d // width
    n_slots = STEP_GROUPS * (s // t)
    grid_spec = pltpu.PrefetchScalarGridSpec(
        num_scalar_prefetch=1,
        grid=(b, N_GROUPS // STEP_GROUPS),
        in_specs=[
            pl.BlockSpec((None, s, width), lambda bi, g, tab: (bi, 0, g)),
            pl.BlockSpec((None, s, width), lambda bi, g, tab: (bi, 0, col0 + g)),
            pl.BlockSpec((None, s, width), lambda bi, g, tab: (bi, 0, 2 * col0 + g)),
            pl.BlockSpec((META_ROWS, width), lambda bi, g, tab: (0, col0 + g)),
            pl.BlockSpec((META_ROWS, width), lambda bi, g, tab: (0, 2 * col0 + g)),
            pl.BlockSpec(u.shape, lambda bi, g, tab: (0, 0)),
            pl.BlockSpec(bias.shape, lambda bi, g, tab: (0, 0, 0)),
        ],
        out_specs=pl.BlockSpec((None, s, width), lambda bi, g, tab: (bi, 0, g)),
        scratch_shapes=[
            pltpu.VMEM((STEP_GROUPS * nh, LANES, s + t), BF16),
            pltpu.VMEM((STEP_GROUPS * nh, s + t, LANES), BF16),
            pltpu.VMEM((2, nh, t, t), F32),
            pltpu.VMEM((2, nh, t, t), F32),
            pltpu.VMEM((2, nh, t, t), BF16),
            pltpu.VMEM((2, nh, t, t), BF16),
            pltpu.VMEM((nh, t, LANES), F32),
            pltpu.VMEM((t, LANES), F32),
            pltpu.VMEM((n_slots, nh, t, LANES), F32),
            pltpu.VMEM((n_slots, t, LANES), F32),
            pltpu.SMEM((n_slots,), jnp.int32),
        ],
    )
    kernel = functools.partial(_attn_pipe_kernel, s=s, t=t, n_iter=n_iter)
    return pl.pallas_call(
        kernel,
        grid_spec=grid_spec,
        out_shape=jax.ShapeDtypeStruct((b, s, d), BF16),
        compiler_params=_params("arbitrary", "arbitrary"),
        name="stick_breaking_attention",
    )(jnp.asarray(table), qkv, qkv, qkv, meta_qkv, meta_qkv, u, bias)


def kernel(x, meta_tokens, conv_norm, conv_w_in, conv_w, conv_w_out, attn_norm,
           attn_w_qkv, attn_w_out, mlp_norm, mlp_w1, mlp_w2, final_norm):
    b, s, d = x.shape
    depth = mlp_norm.shape[0]
    hr = x.reshape(b * s, d)
    hm = jnp.pad(meta_tokens.astype(x.dtype), ((0, META_ROWS - N_META), (0, 0)))
    zero_hist = jnp.zeros((SUBLANES, d), F32)
    gain = lambda g: g.reshape(1, d)

    stacks = {"w_in": conv_w_in, "w_out": conv_w_out, "w_qkv": attn_w_qkv,
              "w_o": attn_w_out, "w1": mlp_w1, "w2": mlp_w2}
    is_conv = lambda i: i % 2 == 0

    def mixer_weights(i):
        return [("w_in", i // 2), ("w_out", i // 2)] if is_conv(i) else [("w_qkv", i // 2)]

    def mlp_weights(i):
        return ([] if is_conv(i) else [("w_o", i // 2)]) + [("w1", i), ("w2", i)]

    first = mixer_weights(0) + ([] if is_conv(0) else mlp_weights(0))
    bf = {key: stacks[key[0]][key[1]].astype(BF16) for key in first}

    def with_casts(call, keys):
        out, *cast = call(casts=[(stacks[name], layer) for name, layer in keys])
        bf.update(zip(keys, cast))
        return out

    for i in range(depth):
        j = i // 2
        last = i == depth - 1
        final_g = gain(final_norm) if last else None
        g_mlp = gain(mlp_norm[i])
        nxt = [] if last else mixer_weights(i + 1) + ([] if is_conv(i + 1) else mlp_weights(i + 1))
        if is_conv(i):
            g = gain(conv_norm[j])
            w_in, w_out = bf["w_in", j], bf["w_out", j]
            hm, hist = _conv_mixer(hm, g, w_in, conv_w[j], w_out, zero_hist,
                                   tm=META_ROWS, tiles_per_seq=1, hist_row=N_META)
            hr = with_casts(functools.partial(
                _conv_mixer, hr, g, w_in, conv_w[j], w_out, hist,
                tm=ROW_TILE, tiles_per_seq=s // ROW_TILE), mlp_weights(i))
            (hm,) = _mlp(hm, g_mlp, bf["w1", i], bf["w2", i], tm=META_ROWS)
            hr = with_casts(functools.partial(
                _mlp, hr, g_mlp, bf["w1", i], bf["w2", i], tm=ROW_TILE, final_g=final_g), nxt)
        else:
            g = gain(attn_norm[j])
            qkv_m = _qkv(hm, g, bf["w_qkv", j], tm=META_ROWS)
            qkv_r = _qkv(hr, g, bf["w_qkv", j], tm=ROW_TILE)
            o_r = _attention(qkv_r.reshape(b, s, 3 * d), qkv_m, t=ATTN_TILE)
            if not last:
                o_m = _attention_small(qkv_m)
                (hm,) = _mlp(hm, g_mlp, bf["w1", i], bf["w2", i], tm=META_ROWS,
                             attn=o_m, w_o=bf["w_o", j])
            hr = with_casts(functools.partial(
                _mlp, hr, g_mlp, bf["w1", i], bf["w2", i], tm=ROW_TILE,
                attn=o_r.reshape(b * s, d), w_o=bf["w_o", j], final_g=final_g), nxt)
    return hr.reshape(b, s, d)
```

```python
import functools
import math

import numpy as np
import jax
import jax.numpy as jnp
from jax import lax
from jax.experimental import pallas as pl
from jax.experimental.pallas import tpu as pltpu

D_MODEL = 1024
N_META = 16
N_HEADS = 16
HEAD_DIM = D_MODEL // N_HEADS
CONV_W = 3
D_FF = 4 * D_MODEL
RMS_EPS = 1e-6

LANES = 128
SUBLANES = 8
HEADS_PER_GROUP = LANES // HEAD_DIM
N_GROUPS = N_HEADS // HEADS_PER_GROUP
META_ROWS = LANES
ROW_TILE = 512
WIDE_ROW_TILE = 1024
ATTN_TILE = 256
FF_CHUNK = 1024
VMEM_LIMIT_BYTES = 56 * 1024 * 1024
LOG2_E = math.log2(math.e)
MASK_BIAS = -1e30

BF16 = jnp.bfloat16
F32 = jnp.float32


def _dot(a, b):
    return jnp.dot(a, b, preferred_element_type=F32)


def _rmsnorm(x, g):
    return x * lax.rsqrt(jnp.mean(x * x, axis=-1, keepdims=True) + RMS_EPS) * g


def _resident(shape):
    zeros = (0,) * len(shape)
    return pl.BlockSpec(shape, lambda *_: zeros, pipeline_mode=pl.Buffered(1))


def _cast_specs(casts, n_steps):
    in_specs, out_specs, out_shapes = [], [], []
    for stack, layer in casts:
        _, rows, cols = stack.shape
        chunk = rows // n_steps
        assert chunk * n_steps == rows and chunk % (2 * SUBLANES) == 0, (rows, n_steps)
        in_specs.append(pl.BlockSpec((None, chunk, cols), lambda i, layer=layer: (layer, i, 0)))
        out_specs.append(pl.BlockSpec((chunk, cols), lambda i: (i, 0)))
        out_shapes.append(jax.ShapeDtypeStruct((rows, cols), BF16))
    return in_specs, out_specs, out_shapes


def _do_casts(src_refs, dst_refs):
    for src, dst in zip(src_refs, dst_refs, strict=True):
        dst[...] = src[...].astype(BF16)


def _params(*semantics):
    return pltpu.CompilerParams(dimension_semantics=semantics,
                                vmem_limit_bytes=VMEM_LIMIT_BYTES)


def _conv_mixer_kernel(x_ref, g_ref, w_in_ref, cw_ref, w_out_ref, hist_ref,
                       *rest, tm, tiles_per_seq, hist_row, n_casts):
    rest = list(rest)
    cast_src = [rest.pop(0) for _ in range(n_casts)]
    o_ref = rest.pop(0)
    hist_out_ref = None if hist_row is None else rest.pop(0)
    cast_dst = [rest.pop(0) for _ in range(n_casts)]
    (ubuf,) = rest
    _do_casts(cast_src, cast_dst)
    i = pl.program_id(0)
    first = (i % tiles_per_seq) == 0

    @pl.when(first)
    def _():
        ubuf[0:SUBLANES, :] = hist_ref[...]

    @pl.when(jnp.logical_not(first))
    def _():
        ubuf[0:SUBLANES, :] = ubuf[tm:tm + SUBLANES, :]

    x = x_ref[...]
    h = _rmsnorm(x, g_ref[...]).astype(BF16)
    d = D_MODEL
    gate_c = _dot(h, w_in_ref[:, d:2 * d])
    val = _dot(h, w_in_ref[:, 2 * d:3 * d])
    ubuf[SUBLANES:SUBLANES + tm, :] = gate_c * val
    if hist_row is not None:
        hist_out_ref[...] = ubuf[hist_row:hist_row + SUBLANES, :]
    cw = cw_ref[...]
    conv = ubuf[SUBLANES:SUBLANES + tm, :] * cw[CONV_W - 1:CONV_W, :]
    for k in range(CONV_W - 1):
        back = CONV_W - 1 - k
        conv = conv + ubuf[SUBLANES - back:SUBLANES - back + tm, :] * cw[k:k + 1, :]
    gate_b = _dot(h, w_in_ref[:, 0:d])
    y = (gate_b * conv).astype(BF16)
    o_ref[...] = x + _dot(y, w_out_ref[...])


def _conv_mixer(x, g, w_in, cw, w_out, hist, *, tm, tiles_per_seq, hist_row=None, casts=()):
    n, d = x.shape
    n_steps = n // tm
    kernel = functools.partial(_conv_mixer_kernel, tm=tm, tiles_per_seq=tiles_per_seq,
                               hist_row=hist_row, n_casts=len(casts))
    row = pl.BlockSpec((tm, d), lambda i: (i, 0))
    cast_in, cast_out, cast_shapes = _cast_specs(casts, n_steps)
    out_shapes = [jax.ShapeDtypeStruct((n, d), F32)]
    out_specs = [row]
    if hist_row is not None:
        out_shapes.append(jax.ShapeDtypeStruct((SUBLANES, d), F32))
        out_specs.append(pl.BlockSpec((SUBLANES, d), lambda i: (0, 0)))
    return pl.pallas_call(
        kernel,
        grid=(n_steps,),
        in_specs=[row, _resident((1, d)), _resident((d, 3 * d)), _resident((CONV_W, d)),
                  _resident((d, d)), _resident((SUBLANES, d))] + cast_in,
        out_specs=out_specs + cast_out,
        out_shape=out_shapes + cast_shapes,
        scratch_shapes=[pltpu.VMEM((tm + 2 * SUBLANES, d), F32)],
        compiler_params=_params("arbitrary"),
        name="conv_mixer",
    )(x, g, w_in, cw, w_out, hist, *[stack for stack, _ in casts])


def _mlp_kernel(*refs, with_proj, with_final_norm, n_casts):
    refs = list(refs)
    x_ref = refs.pop(0)
    if with_proj:
        a_ref, w_o_ref = refs.pop(0), refs.pop(0)
    g_ref, w1_ref, w2_ref = refs.pop(0), refs.pop(0), refs.pop(0)
    if with_final_norm:
        gf_ref = refs.pop(0)
    cast_src = [refs.pop(0) for _ in range(n_casts)]
    o_ref = refs.pop(0)
    _do_casts(cast_src, refs)

    x = x_ref[...]
    if with_proj:
        x = x + _dot(a_ref[...], w_o_ref[...])
    h = _rmsnorm(x, g_ref[...]).astype(BF16)
    acc = x
    for c in range(D_FF // FF_CHUNK):
        cols = slice(c * FF_CHUNK, (c + 1) * FF_CHUNK)
        a = jnp.maximum(_dot(h, w1_ref[:, cols]), 0.0)
        acc = acc + _dot((a * a).astype(BF16), w2_ref[cols, :])
    if with_final_norm:
        acc = _rmsnorm(acc, gf_ref[...])
    o_ref[...] = acc


def _mlp(x, g, w1, w2, *, tm, attn=None, w_o=None, final_g=None, casts=()):
    n, d = x.shape
    n_steps = n // tm
    with_proj = attn is not None
    with_final_norm = final_g is not None
    row = pl.BlockSpec((tm, d), lambda i: (i, 0))
    args, specs = [x], [row]
    if with_proj:
        args += [attn, w_o]
        specs += [row, _resident((d, d))]
    args += [g, w1, w2]
    specs += [_resident((1, d)), _resident((d, D_FF)), _resident((D_FF, d))]
    if with_final_norm:
        args.append(final_g)
        specs.append(_resident((1, d)))
    cast_in, cast_out, cast_shapes = _cast_specs(casts, n_steps)
    kernel = functools.partial(_mlp_kernel, with_proj=with_proj,
                               with_final_norm=with_final_norm, n_casts=len(casts))
    return pl.pallas_call(
        kernel,
        grid=(n_steps,),
        in_specs=specs + cast_in,
        out_specs=[row] + cast_out,
        out_shape=[jax.ShapeDtypeStruct((n, d), F32)] + cast_shapes,
        compiler_params=_params("arbitrary"),
        name="mlp",
    )(*args, *[stack for stack, _ in casts])


def _qkv_kernel(x_ref, g_ref, w_qkv_ref, qkv_ref):
    d = D_MODEL
    h = _rmsnorm(x_ref[...], g_ref[...]).astype(BF16)
    scale = 1.0 / math.sqrt(HEAD_DIM)
    qkv_ref[:, 0:d] = (_dot(h, w_qkv_ref[:, 0:d]) * scale).astype(BF16)
    qkv_ref[:, d:3 * d] = _dot(h, w_qkv_ref[:, d:3 * d]).astype(BF16)


def _qkv(x, g, w_qkv, *, tm):
    n, d = x.shape
    return pl.pallas_call(
        _qkv_kernel,
        grid=(n // tm,),
        in_specs=[pl.BlockSpec((tm, d), lambda i: (i, 0)), _resident((1, d)),
                  _resident((d, 3 * d))],
        out_specs=pl.BlockSpec((tm, 3 * d), lambda i: (i, 0)),
        out_shape=jax.ShapeDtypeStruct((n, 3 * d), BF16),
        compiler_params=_params("arbitrary"),
        name="qkv_proj",
    )(x, g, w_qkv)


def _suffix_sum_matrix(t, sign):
    r = jnp.arange(t)[:, None]
    c = jnp.arange(t + LANES)[None, :]
    return jnp.where((c >= t) | (r > c), sign, 0.0).astype(BF16)


def _head_of_lane(shape):
    return lax.broadcasted_iota(jnp.int32, shape, len(shape) - 1) // HEAD_DIM


def _attn_small_kernel(q_ref, k_ref, v_ref, u_ref, o_ref, *, t):
    q = q_ref[...]
    k = k_ref[...]
    head_of_lane = _head_of_lane((t, LANES))
    row_i = lax.broadcasted_iota(jnp.int32, (t, t), 0)
    col_i = lax.broadcasted_iota(jnp.int32, (t, t), 1)
    causal = col_i < row_i
    out = None
    for hh in range(HEADS_PER_GROUP):
        qh = jnp.where(head_of_lane == hh, q, jnp.zeros_like(q))
        z = lax.dot_general(qh, k, (((1,), (1,)), ((), ())), preferred_element_type=F32)
        soft = jnp.log(1.0 + jnp.exp(-jnp.abs(z)))
        log_beta = jnp.minimum(z, 0.0) - soft
        log_1m = jnp.where(causal, log_beta - z, 0.0)
        tail = _dot(log_1m.astype(BF16), u_ref[:, 0:t])
        w = jnp.where(causal, jnp.exp(log_beta + tail), 0.0)
        oh = _dot(w.astype(BF16), v_ref[...])
        out = oh if out is None else jnp.where(head_of_lane == hh, oh, out)
    o_ref[...] = out.astype(BF16)


def _attention_small(qkv):
    t = qkv.shape[0]
    d = D_MODEL
    u = _suffix_sum_matrix(t, 1.0)
    col0 = d // LANES
    return pl.pallas_call(
        functools.partial(_attn_small_kernel, t=t),
        grid=(N_GROUPS,),
        in_specs=[pl.BlockSpec((t, LANES), lambda g: (0, g)),
                  pl.BlockSpec((t, LANES), lambda g: (0, col0 + g)),
                  pl.BlockSpec((t, LANES), lambda g: (0, 2 * col0 + g)),
                  _resident(u.shape)],
        out_specs=pl.BlockSpec((t, LANES), lambda g: (0, g)),
        out_shape=jax.ShapeDtypeStruct((t, d), BF16),
        compiler_params=_params("arbitrary"),
        name="attention_meta",
    )(qkv, qkv, qkv, u)


_KIND_PLAIN, _KIND_DIAG, _KIND_META = 0, 1, 2
_PIPE_DEPTH = 4
_UNROLL = 34
_ITEM_FIELDS = 6
_PIPELINED_BLOCKS = 2
assert _PIPELINED_BLOCKS == 2 and _PIPE_DEPTH % 2 == 0 and _UNROLL % 2 == 0
STEP_GROUPS = 4
LOG_UNDERFLOW = -105.0


def _tile_blocks(qi, s, t):
    blocks = [(qi * t, _KIND_DIAG)]
    blocks += [(j * t, _KIND_PLAIN) for j in range(qi - 1, -1, -1)]
    blocks.append((s, _KIND_META))
    return blocks


def _attention_items(s, t):
    n_tiles = s // t
    items = []
    for grp in range(STEP_GROUPS):
        for qi in range(n_tiles):
            for k_off, kind in _tile_blocks(qi, s, t)[:_PIPELINED_BLOCKS]:
                items.append([qi * t, k_off, kind, grp * n_tiles + qi, grp, qi * t])
    lead = _PIPE_DEPTH - 1
    n_iter = len(items) + lead
    n_iter += -n_iter % _UNROLL
    first_pair, last_pair = items[:2], items[-2:]
    table = (first_pair * 2)[-lead:] + items
    while len(table) < n_iter + lead:
        table.append(last_pair[(len(table) - lead) % 2])
    return np.asarray(table, np.int32).reshape(-1), n_iter


def _mask_bias(t):
    r = np.arange(t)[:, None]
    c = np.arange(t)[None, :]
    bias = np.zeros((3, t, t), np.float32)
    bias[_KIND_DIAG] = np.where(c < r, 0.0, MASK_BIAS)
    bias[_KIND_META] = np.where(c < N_META, 0.0, MASK_BIAS) + 0.0 * r
    return jnp.asarray(bias)


def _attn_pipe_kernel(tab_ref, q_ref, k_ref, v_ref, km_ref, vm_ref, u_ref, bias_ref,
                      o_ref, kt_all, v_all, zbuf, lbuf, pbuf, wbuf, carry_ref,
                      acc_ref, carry_all, acc_all, open_ref, *, s, t, n_iter):
    nh = HEADS_PER_GROUP
    first_step = jnp.logical_and(pl.program_id(0) == 0, pl.program_id(1) == 0)

    @pl.when(first_step)
    def _():
        for ref in (zbuf, lbuf, pbuf, wbuf, carry_ref, acc_ref):
            ref[...] = jnp.zeros_like(ref)
        kt_all[...] = jnp.zeros_like(kt_all)
        v_all[:, s + META_ROWS:s + t, :] = jnp.zeros(
            (STEP_GROUPS * nh, t - META_ROWS, LANES), BF16)

    head_of_lane = _head_of_lane((1, LANES))
    for grp in range(STEP_GROUPS):
        cols = slice(grp * LANES, (grp + 1) * LANES)
        v = v_ref[:, cols]
        vm = vm_ref[:, cols]
        for hh in range(nh):
            mine = head_of_lane == hh
            v_all[grp * nh + hh, 0:s, :] = jnp.where(mine, v, jnp.zeros_like(v))
            v_all[grp * nh + hh, s:s + META_ROWS, :] = jnp.where(mine, vm, jnp.zeros_like(vm))
        chunks = [(k_ref, c, c) for c in range(0, s, t)] + [(km_ref, 0, s)]
        for src_ref, src_row, dst_col in chunks:
            n_keys = min(t, src_ref.shape[0])
            kt = src_ref[src_row:src_row + n_keys, cols].T
            for hh in range(nh):
                rows = slice(hh * HEAD_DIM, (hh + 1) * HEAD_DIM)
                kt_all[grp * nh + hh, rows, dst_col:dst_col + n_keys] = kt[rows, :]

    def stage_scores(q_off, k_off, grp, p):
        q_blk = q_ref[pl.ds(pl.multiple_of(q_off, t), t),
                      pl.ds(pl.multiple_of(grp * LANES, LANES), LANES)]
        for hh in range(nh):
            zbuf[p, hh] = _dot(q_blk, kt_all[grp * nh + hh, :, pl.ds(pl.multiple_of(k_off, t), t)])

    def stage_logs(kind, p):
        bias = bias_ref[kind]
        for hh in range(nh):
            z = zbuf[1 - p, hh] + bias
            soft = jnp.log(1.0 + jnp.exp2(jnp.abs(z) * (-LOG2_E)))
            neg_log_1m = jnp.maximum(z, 0.0) + soft
            lbuf[p, hh] = z - neg_log_1m
            pbuf[p, hh] = neg_log_1m.astype(BF16)

    def stage_weights(first, p):
        for hh in range(nh):
            sums = _dot(pbuf[1 - p, hh], u_ref[...])
            tail, row_sum = sums[:, 0:t], sums[:, t:t + LANES]
            if first:
                carry_ref[hh] = row_sum
            else:
                carry = carry_ref[hh]
                tail = tail + jnp.concatenate([carry] * (t // LANES), axis=1)
                carry_ref[hh] = carry + row_sum
            wbuf[p, hh] = jnp.exp(lbuf[1 - p, hh] + tail).astype(BF16)

    def stage_output(q_off, k_off, grp, first, p):
        k_off = pl.multiple_of(k_off, t)
        w_cat = jnp.concatenate([wbuf[1 - p, hh] for hh in range(nh)], axis=1)
        v_cat = jnp.concatenate([v_all[grp * nh + hh, pl.ds(k_off, t), :]
                                 for hh in range(nh)], axis=0)
        acc = _dot(w_cat, v_cat)
        if not first:
            acc = acc_ref[...] + acc
            o_ref[pl.ds(pl.multiple_of(q_off, t), t),
                  pl.ds(pl.multiple_of(grp * LANES, LANES), LANES)] = acc.astype(BF16)
        acc_ref[...] = acc

    def field(item, f):
        return tab_ref[item * _ITEM_FIELDS + f]

    def iteration(i, parity):
        p = parity
        stage_logs(field(i + 2, 2), p)
        if parity == 0:
            stage_weights(True, p)
        else:
            stage_weights(False, p)
            carry_all[field(i + 1, 3)] = carry_ref[...]
        stage_scores(field(i + 3, 5), field(i + 3, 1), field(i + 3, 4), p)
        stage_output(field(i, 0), field(i, 1), field(i, 4), parity == 1, p)
        if parity == 0:
            acc_all[field(i, 3)] = acc_ref[...]

    def body(k, c):
        for j in range(_UNROLL):
            iteration(_UNROLL * k + j, j % 2)
        return c

    lax.fori_loop(0, n_iter // _UNROLL, body, 0)

    def saturated():
        return jnp.max(carry_ref[...]) <= LOG_UNDERFLOW

    n_tiles = s // t
    first_open = _PIPELINED_BLOCKS - 1

    def finish_tile(slot, c):
        grp = slot // n_tiles
        qi = slot % n_tiles
        n_blocks = qi + 2

        @pl.when(open_ref[slot] != 0)
        def _():
            carry_ref[...] = carry_all[slot]
            acc_ref[...] = acc_all[slot]
            q_off = qi * t

            def more(state):
                n, done = state
                return jnp.logical_and(n < n_blocks, jnp.logical_not(done))

            def one_block(state):
                n, _ = state
                is_meta = n == n_blocks - 1
                k_off = jnp.where(is_meta, s, (qi - n) * t)
                kind = jnp.where(is_meta, _KIND_META, _KIND_PLAIN)
                stage_scores(q_off, k_off, grp, 0)
                stage_logs(kind, 1)
                stage_weights(None, 0)
                stage_output(q_off, k_off, grp, None, 1)
                return n + 1, saturated()

            lax.while_loop(more, one_block, (jnp.int32(_PIPELINED_BLOCKS), jnp.bool_(False)))

        return c

    open_slots = [grp * n_tiles + qi for grp in range(STEP_GROUPS)
                  for qi in range(first_open, n_tiles)]
    worst = carry_all[open_slots[0]]
    for slot in open_slots[1:]:
        worst = jnp.maximum(worst, carry_all[slot])

    @pl.when(jnp.max(worst) > LOG_UNDERFLOW)
    def _():
        for slot in range(STEP_GROUPS * n_tiles):
            if slot in open_slots:
                open_ref[slot] = (jnp.max(carry_all[slot]) > LOG_UNDERFLOW).astype(jnp.int32)
            else:
                open_ref[slot] = 0
        lax.fori_loop(0, STEP_GROUPS * n_tiles, finish_tile, 0)


def _attention(qkv, meta_qkv, *, t):
    b, s, _ = qkv.shape
    d = D_MODEL
    nh = HEADS_PER_GROUP
    table, n_iter = _attention_items(s, t)
    u = _suffix_sum_matrix(t, -1.0)
    bias = _mask_bias(t)
    width = STEP_GROUPS * LANES
    col0 = d // width
    n_slots = STEP_GROUPS * (s // t)
    grid_spec = pltpu.PrefetchScalarGridSpec(
        num_scalar_prefetch=1,
        grid=(b, N_GROUPS // STEP_GROUPS),
        in_specs=[
            pl.BlockSpec((None, s, width), lambda bi, g, tab: (bi, 0, g)),
            pl.BlockSpec((None, s, width), lambda bi, g, tab: (bi, 0, col0 + g)),
            pl.BlockSpec((None, s, width), lambda bi, g, tab: (bi, 0, 2 * col0 + g)),
            pl.BlockSpec((META_ROWS, width), lambda bi, g, tab: (0, col0 + g)),
            pl.BlockSpec((META_ROWS, width), lambda bi, g, tab: (0, 2 * col0 + g)),
            pl.BlockSpec(u.shape, lambda bi, g, tab: (0, 0)),
            pl.BlockSpec(bias.shape, lambda bi, g, tab: (0, 0, 0)),
        ],
        out_specs=pl.BlockSpec((None, s, width), lambda bi, g, tab: (bi, 0, g)),
        scratch_shapes=[
            pltpu.VMEM((STEP_GROUPS * nh, LANES, s + t), BF16),
            pltpu.VMEM((STEP_GROUPS * nh, s + t, LANES), BF16),
            pltpu.VMEM((2, nh, t, t), F32),
            pltpu.VMEM((2, nh, t, t), F32),
            pltpu.VMEM((2, nh, t, t), BF16),
            pltpu.VMEM((2, nh, t, t), BF16),
            pltpu.VMEM((nh, t, LANES), F32),
            pltpu.VMEM((t, LANES), F32),
            pltpu.VMEM((n_slots, nh, t, LANES), F32),
            pltpu.VMEM((n_slots, t, LANES), F32),
            pltpu.SMEM((n_slots,), jnp.int32),
        ],
    )
    kernel = functools.partial(_attn_pipe_kernel, s=s, t=t, n_iter=n_iter)
    return pl.pallas_call(
        kernel,
        grid_spec=grid_spec,
        out_shape=jax.ShapeDtypeStruct((b, s, d), BF16),
        compiler_params=_params("arbitrary", "arbitrary"),
        name="stick_breaking_attention",
    )(jnp.asarray(table), qkv, qkv, qkv, meta_qkv, meta_qkv, u, bias)


def kernel(x, meta_tokens, conv_norm, conv_w_in, conv_w, conv_w_out, attn_norm,
           attn_w_qkv, attn_w_out, mlp_norm, mlp_w1, mlp_w2, final_norm):
    b, s, d = x.shape
    depth = mlp_norm.shape[0]
    hr = x.reshape(b * s, d)
    hm = jnp.pad(meta_tokens.astype(x.dtype), ((0, META_ROWS - N_META), (0, 0)))
    zero_hist = jnp.zeros((SUBLANES, d), F32)
    gain = lambda g: g.reshape(1, d)

    stacks = {"w_in": conv_w_in, "w_out": conv_w_out, "w_qkv": attn_w_qkv,
              "w_o": attn_w_out, "w1": mlp_w1, "w2": mlp_w2}
    is_conv = lambda i: i % 2 == 0

    def mixer_weights(i):
        return [("w_in", i // 2), ("w_out", i // 2)] if is_conv(i) else [("w_qkv", i // 2)]

    def mlp_weights(i):
        return ([] if is_conv(i) else [("w_o", i // 2)]) + [("w1", i), ("w2", i)]

    first = mixer_weights(0) + ([] if is_conv(0) else mlp_weights(0))
    bf = {key: stacks[key[0]][key[1]].astype(BF16) for key in first}

    def with_casts(call, keys):
        out, *cast = call(casts=[(stacks[name], layer) for name, layer in keys])
        bf.update(zip(keys, cast))
        return out

    for i in range(depth):
        j = i // 2
        last = i == depth - 1
        final_g = gain(final_norm) if last else None
        g_mlp = gain(mlp_norm[i])
        nxt = [] if last else mixer_weights(i + 1) + ([] if is_conv(i + 1) else mlp_weights(i + 1))
        if is_conv(i):
            g = gain(conv_norm[j])
            w_in, w_out = bf["w_in", j], bf["w_out", j]
            hm, hist = _conv_mixer(hm, g, w_in, conv_w[j], w_out, zero_hist,
                                   tm=META_ROWS, tiles_per_seq=1, hist_row=N_META)
            hr = with_casts(functools.partial(
                _conv_mixer, hr, g, w_in, conv_w[j], w_out, hist,
                tm=WIDE_ROW_TILE, tiles_per_seq=s // WIDE_ROW_TILE), mlp_weights(i))
            (hm,) = _mlp(hm, g_mlp, bf["w1", i], bf["w2", i], tm=META_ROWS)
            hr = with_casts(functools.partial(
                _mlp, hr, g_mlp, bf["w1", i], bf["w2", i], tm=ROW_TILE, final_g=final_g), nxt)
        else:
            g = gain(attn_norm[j])
            qkv_m = _qkv(hm, g, bf["w_qkv", j], tm=META_ROWS)
            qkv_r = _qkv(hr, g, bf["w_qkv", j], tm=WIDE_ROW_TILE)
            o_r = _attention(qkv_r.reshape(b, s, 3 * d), qkv_m, t=ATTN_TILE)
            if not last:
                o_m = _attention_small(qkv_m)
                (hm,) = _mlp(hm, g_mlp, bf["w1", i], bf["w2", i], tm=META_ROWS,
                             attn=o_m, w_o=bf["w_o", j])
            hr = with_casts(functools.partial(
                _mlp, hr, g_mlp, bf["w1", i], bf["w2", i], tm=ROW_TILE,
                attn=o_r.reshape(b * s, d), w_o=bf["w_o", j], final_g=final_g), nxt)
    return hr.reshape(b, s, d)
```

```python
import functools
import math

import numpy as np
import jax
import jax.numpy as jnp
from jax import lax
from jax.experimental import pallas as pl
from jax.experimental.pallas import tpu as pltpu

D_MODEL = 1024
N_META = 16
N_HEADS = 16
HEAD_DIM = D_MODEL // N_HEADS
CONV_W = 3
D_FF = 4 * D_MODEL
RMS_EPS = 1e-6

LANES = 128
SUBLANES = 8
HEADS_PER_GROUP = LANES // HEAD_DIM
N_GROUPS = N_HEADS // HEADS_PER_GROUP
META_ROWS = LANES
ROW_TILE = 512
WIDE_ROW_TILE = 1024
ATTN_TILE = 256
FF_CHUNK = 1024
VMEM_LIMIT_BYTES = 56 * 1024 * 1024
LOG2_E = math.log2(math.e)
MASK_BIAS = -1e30

BF16 = jnp.bfloat16
F32 = jnp.float32


def _dot(a, b):
    return jnp.dot(a, b, preferred_element_type=F32)


def _rmsnorm(x, g):
    return x * lax.rsqrt(jnp.mean(x * x, axis=-1, keepdims=True) + RMS_EPS) * g


def _resident(shape):
    zeros = (0,) * len(shape)
    return pl.BlockSpec(shape, lambda *_: zeros, pipeline_mode=pl.Buffered(1))


def _cast_specs(casts, n_steps):
    in_specs, out_specs, out_shapes = [], [], []
    for stack, layer in casts:
        _, rows, cols = stack.shape
        chunk = rows // n_steps
        assert chunk * n_steps == rows and chunk % (2 * SUBLANES) == 0, (rows, n_steps)
        in_specs.append(pl.BlockSpec((None, chunk, cols), lambda i, layer=layer: (layer, i, 0)))
        out_specs.append(pl.BlockSpec((chunk, cols), lambda i: (i, 0)))
        out_shapes.append(jax.ShapeDtypeStruct((rows, cols), BF16))
    return in_specs, out_specs, out_shapes


def _do_casts(src_refs, dst_refs):
    for src, dst in zip(src_refs, dst_refs, strict=True):
        dst[...] = src[...].astype(BF16)


def _params(*semantics):
    return pltpu.CompilerParams(dimension_semantics=semantics,
                                vmem_limit_bytes=VMEM_LIMIT_BYTES)


def _conv_mixer_kernel(x_ref, g_ref, w_in_ref, cw_ref, w_out_ref, hist_ref,
                       *rest, tm, tiles_per_seq, hist_row, n_casts):
    rest = list(rest)
    cast_src = [rest.pop(0) for _ in range(n_casts)]
    o_ref = rest.pop(0)
    hist_out_ref = None if hist_row is None else rest.pop(0)
    cast_dst = [rest.pop(0) for _ in range(n_casts)]
    (ubuf,) = rest
    _do_casts(cast_src, cast_dst)
    i = pl.program_id(0)
    first = (i % tiles_per_seq) == 0

    @pl.when(first)
    def _():
        ubuf[0:SUBLANES, :] = hist_ref[...]

    @pl.when(jnp.logical_not(first))
    def _():
        ubuf[0:SUBLANES, :] = ubuf[tm:tm + SUBLANES, :]

    x = x_ref[...]
    h = _rmsnorm(x, g_ref[...]).astype(BF16)
    d = D_MODEL
    gate_c = _dot(h, w_in_ref[:, d:2 * d])
    val = _dot(h, w_in_ref[:, 2 * d:3 * d])
    ubuf[SUBLANES:SUBLANES + tm, :] = gate_c * val
    if hist_row is not None:
        hist_out_ref[...] = ubuf[hist_row:hist_row + SUBLANES, :]
    cw = cw_ref[...]
    conv = ubuf[SUBLANES:SUBLANES + tm, :] * cw[CONV_W - 1:CONV_W, :]
    for k in range(CONV_W - 1):
        back = CONV_W - 1 - k
        conv = conv + ubuf[SUBLANES - back:SUBLANES - back + tm, :] * cw[k:k + 1, :]
    gate_b = _dot(h, w_in_ref[:, 0:d])
    y = (gate_b * conv).astype(BF16)
    o_ref[...] = x + _dot(y, w_out_ref[...])


def _conv_mixer(x, g, w_in, cw, w_out, hist, *, tm, tiles_per_seq, hist_row=None, casts=()):
    n, d = x.shape
    n_steps = n // tm
    kernel = functools.partial(_conv_mixer_kernel, tm=tm, tiles_per_seq=tiles_per_seq,
                               hist_row=hist_row, n_casts=len(casts))
    row = pl.BlockSpec((tm, d), lambda i: (i, 0))
    cast_in, cast_out, cast_shapes = _cast_specs(casts, n_steps)
    out_shapes = [jax.ShapeDtypeStruct((n, d), F32)]
    out_specs = [row]
    if hist_row is not None:
        out_shapes.append(jax.ShapeDtypeStruct((SUBLANES, d), F32))
        out_specs.append(pl.BlockSpec((SUBLANES, d), lambda i: (0, 0)))
    return pl.pallas_call(
        kernel,
        grid=(n_steps,),
        in_specs=[row, _resident((1, d)), _resident((d, 3 * d)), _resident((CONV_W, d)),
                  _resident((d, d)), _resident((SUBLANES, d))] + cast_in,
        out_specs=out_specs + cast_out,
        out_shape=out_shapes + cast_shapes,
        scratch_shapes=[pltpu.VMEM((tm + 2 * SUBLANES, d), F32)],
        compiler_params=_params("arbitrary"),
        name="conv_mixer",
    )(x, g, w_in, cw, w_out, hist, *[stack for stack, _ in casts])


def _mlp_kernel(*refs, with_proj, with_final_norm, n_casts):
    refs = list(refs)
    x_ref = refs.pop(0)
    if with_proj:
        a_ref, w_o_ref = refs.pop(0), refs.pop(0)
    g_ref, w1_ref, w2_ref = refs.pop(0), refs.pop(0), refs.pop(0)
    if with_final_norm:
        gf_ref = refs.pop(0)
    cast_src = [refs.pop(0) for _ in range(n_casts)]
    o_ref = refs.pop(0)
    _do_casts(cast_src, refs)

    x = x_ref[...]
    if with_proj:
        x = x + _dot(a_ref[...], w_o_ref[...])
    h = _rmsnorm(x, g_ref[...]).astype(BF16)
    acc = x
    for c in range(D_FF // FF_CHUNK):
        cols = slice(c * FF_CHUNK, (c + 1) * FF_CHUNK)
        a = jnp.maximum(_dot(h, w1_ref[:, cols]), 0.0)
        acc = acc + _dot((a * a).astype(BF16), w2_ref[cols, :])
    if with_final_norm:
        acc = _rmsnorm(acc, gf_ref[...])
    o_ref[...] = acc


def _mlp(x, g, w1, w2, *, tm, attn=None, w_o=None, final_g=None, casts=()):
    n, d = x.shape
    n_steps = n // tm
    with_proj = attn is not None
    with_final_norm = final_g is not None
    row = pl.BlockSpec((tm, d), lambda i: (i, 0))
    args, specs = [x], [row]
    if with_proj:
        args += [attn, w_o]
        specs += [row, _resident((d, d))]
    args += [g, w1, w2]
    specs += [_resident((1, d)), _resident((d, D_FF)), _resident((D_FF, d))]
    if with_final_norm:
        args.append(final_g)
        specs.append(_resident((1, d)))
    cast_in, cast_out, cast_shapes = _cast_specs(casts, n_steps)
    kernel = functools.partial(_mlp_kernel, with_proj=with_proj,
                               with_final_norm=with_final_norm, n_casts=len(casts))
    return pl.pallas_call(
        kernel,
        grid=(n_steps,),
        in_specs=specs + cast_in,
        out_specs=[row] + cast_out,
        out_shape=[jax.ShapeDtypeStruct((n, d), F32)] + cast_shapes,
        compiler_params=_params("arbitrary"),
        name="mlp",
    )(*args, *[stack for stack, _ in casts])


def _qkv_kernel(x_ref, g_ref, w_qkv_ref, qkv_ref):
    d = D_MODEL
    h = _rmsnorm(x_ref[...], g_ref[...]).astype(BF16)
    scale = 1.0 / math.sqrt(HEAD_DIM)
    qkv_ref[:, 0:d] = (_dot(h, w_qkv_ref[:, 0:d]) * scale).astype(BF16)
    qkv_ref[:, d:3 * d] = _dot(h, w_qkv_ref[:, d:3 * d]).astype(BF16)


def _qkv(x, g, w_qkv, *, tm):
    n, d = x.shape
    return pl.pallas_call(
        _qkv_kernel,
        grid=(n // tm,),
        in_specs=[pl.BlockSpec((tm, d), lambda i: (i, 0)), _resident((1, d)),
                  _resident((d, 3 * d))],
        out_specs=pl.BlockSpec((tm, 3 * d), lambda i: (i, 0)),
        out_shape=jax.ShapeDtypeStruct((n, 3 * d), BF16),
        compiler_params=_params("arbitrary"),
        name="qkv_proj",
    )(x, g, w_qkv)


def _suffix_sum_matrix(t, sign):
    r = jnp.arange(t)[:, None]
    c = jnp.arange(t + LANES)[None, :]
    return jnp.where((c >= t) | (r > c), sign, 0.0).astype(BF16)


def _head_of_lane(shape):
    return lax.broadcasted_iota(jnp.int32, shape, len(shape) - 1) // HEAD_DIM


def _attn_small_kernel(q_ref, k_ref, v_ref, u_ref, o_ref, *, t):
    q = q_ref[...]
    k = k_ref[...]
    head_of_lane = _head_of_lane((t, LANES))
    row_i = lax.broadcasted_iota(jnp.int32, (t, t), 0)
    col_i = lax.broadcasted_iota(jnp.int32, (t, t), 1)
    causal = col_i < row_i
    out = None
    for hh in range(HEADS_PER_GROUP):
        qh = jnp.where(head_of_lane == hh, q, jnp.zeros_like(q))
        z = lax.dot_general(qh, k, (((1,), (1,)), ((), ())), preferred_element_type=F32)
        soft = jnp.log(1.0 + jnp.exp(-jnp.abs(z)))
        log_beta = jnp.minimum(z, 0.0) - soft
        log_1m = jnp.where(causal, log_beta - z, 0.0)
        tail = _dot(log_1m.astype(BF16), u_ref[:, 0:t])
        w = jnp.where(causal, jnp.exp(log_beta + tail), 0.0)
        oh = _dot(w.astype(BF16), v_ref[...])
        out = oh if out is None else jnp.where(head_of_lane == hh, oh, out)
    o_ref[...] = out.astype(BF16)


def _attention_small(qkv):
    t = qkv.shape[0]
    d = D_MODEL
    u = _suffix_sum_matrix(t, 1.0)
    col0 = d // LANES
    return pl.pallas_call(
        functools.partial(_attn_small_kernel, t=t),
        grid=(N_GROUPS,),
        in_specs=[pl.BlockSpec((t, LANES), lambda g: (0, g)),
                  pl.BlockSpec((t, LANES), lambda g: (0, col0 + g)),
                  pl.BlockSpec((t, LANES), lambda g: (0, 2 * col0 + g)),
                  _resident(u.shape)],
        out_specs=pl.BlockSpec((t, LANES), lambda g: (0, g)),
        out_shape=jax.ShapeDtypeStruct((t, d), BF16),
        compiler_params=_params("arbitrary"),
        name="attention_meta",
    )(qkv, qkv, qkv, u)


_KIND_PLAIN, _KIND_DIAG, _KIND_META = 0, 1, 2
_PIPE_DEPTH = 4
_UNROLL = 34
_ITEM_FIELDS = 6
_PIPELINED_BLOCKS = 2
assert _PIPELINED_BLOCKS == 2 and _PIPE_DEPTH % 2 == 0 and _UNROLL % 2 == 0
STEP_GROUPS = 4
LOG_UNDERFLOW = -105.0


def _tile_blocks(qi, s, t):
    blocks = [(qi * t, _KIND_DIAG)]
    blocks += [(j * t, _KIND_PLAIN) for j in range(qi - 1, -1, -1)]
    blocks.append((s, _KIND_META))
    return blocks


def _attention_items(s, t):
    n_tiles = s // t
    items = []
    for grp in range(STEP_GROUPS):
        for qi in range(n_tiles):
            for k_off, kind in _tile_blocks(qi, s, t)[:_PIPELINED_BLOCKS]:
                items.append([qi * t, k_off, kind, grp * n_tiles + qi, grp, qi * t])
    lead = _PIPE_DEPTH - 1
    n_iter = len(items) + lead
    n_iter += -n_iter % _UNROLL
    first_pair, last_pair = items[:2], items[-2:]
    table = (first_pair * 2)[-lead:] + items
    while len(table) < n_iter + lead:
        table.append(last_pair[(len(table) - lead) % 2])
    return np.asarray(table, np.int32).reshape(-1), n_iter


def _mask_bias(t):
    r = np.arange(t)[:, None]
    c = np.arange(t)[None, :]
    bias = np.zeros((3, t, t), np.float32)
    bias[_KIND_DIAG] = np.where(c < r, 0.0, MASK_BIAS)
    bias[_KIND_META] = np.where(c < N_META, 0.0, MASK_BIAS) + 0.0 * r
    return jnp.asarray(bias)


def _attn_pipe_kernel(tab_ref, q_ref, k_ref, v_ref, km_ref, vm_ref, u_ref, bias_ref,
                      o_ref, kt_all, v_all, zbuf, lbuf, pbuf, wbuf, carry_ref,
                      acc_ref, carry_all, acc_all, open_ref, *, s, t, n_iter):
    nh = HEADS_PER_GROUP
    first_step = jnp.logical_and(pl.program_id(0) == 0, pl.program_id(1) == 0)

    @pl.when(first_step)
    def _():
        for ref in (zbuf, lbuf, pbuf, wbuf, carry_ref, acc_ref):
            ref[...] = jnp.zeros_like(ref)
        kt_all[...] = jnp.zeros_like(kt_all)
        v_all[:, s + META_ROWS:s + t, :] = jnp.zeros(
            (STEP_GROUPS * nh, t - META_ROWS, LANES), BF16)

    head_of_lane = _head_of_lane((1, LANES))
    for grp in range(STEP_GROUPS):
        cols = slice(grp * LANES, (grp + 1) * LANES)
        v = v_ref[:, cols]
        vm = vm_ref[:, cols]
        for hh in range(nh):
            mine = head_of_lane == hh
            v_all[grp * nh + hh, 0:s, :] = jnp.where(mine, v, jnp.zeros_like(v))
            v_all[grp * nh + hh, s:s + META_ROWS, :] = jnp.where(mine, vm, jnp.zeros_like(vm))
        chunks = [(k_ref, c, c) for c in range(0, s, t)] + [(km_ref, 0, s)]
        for src_ref, src_row, dst_col in chunks:
            n_keys = min(t, src_ref.shape[0])
            kt = src_ref[src_row:src_row + n_keys, cols].T
            for hh in range(nh):
                rows = slice(hh * HEAD_DIM, (hh + 1) * HEAD_DIM)
                kt_all[grp * nh + hh, rows, dst_col:dst_col + n_keys] = kt[rows, :]

    def stage_scores(q_off, k_off, grp, p):
        q_blk = q_ref[pl.ds(pl.multiple_of(q_off, t), t),
                      pl.ds(pl.multiple_of(grp * LANES, LANES), LANES)]
        for hh in range(nh):
            zbuf[p, hh] = _dot(q_blk, kt_all[grp * nh + hh, :, pl.ds(pl.multiple_of(k_off, t), t)])

    def stage_logs(kind, p):
        bias = bias_ref[kind]
        for hh in range(nh):
            z = zbuf[1 - p, hh] + bias
            soft = jnp.log(1.0 + jnp.exp2(jnp.abs(z) * (-LOG2_E)))
            neg_log_1m = jnp.maximum(z, 0.0) + soft
            lbuf[p, hh] = z - neg_log_1m
            pbuf[p, hh] = neg_log_1m.astype(BF16)

    def stage_weights(first, p):
        for hh in range(nh):
            sums = _dot(pbuf[1 - p, hh], u_ref[...])
            tail, row_sum = sums[:, 0:t], sums[:, t:t + LANES]
            if first:
                carry_ref[hh] = row_sum
            else:
                carry = carry_ref[hh]
                tail = tail + jnp.concatenate([carry] * (t // LANES), axis=1)
                carry_ref[hh] = carry + row_sum
            wbuf[p, hh] = jnp.exp(lbuf[1 - p, hh] + tail).astype(BF16)

    def stage_output(q_off, k_off, grp, first, p):
        k_off = pl.multiple_of(k_off, t)
        w_cat = jnp.concatenate([wbuf[1 - p, hh] for hh in range(nh)], axis=1)
        v_cat = jnp.concatenate([v_all[grp * nh + hh, pl.ds(k_off, t), :]
                                 for hh in range(nh)], axis=0)
        acc = _dot(w_cat, v_cat)
        if not first:
            acc = acc_ref[...] + acc
            o_ref[pl.ds(pl.multiple_of(q_off, t), t),
                  pl.ds(pl.multiple_of(grp * LANES, LANES), LANES)] = acc.astype(BF16)
        acc_ref[...] = acc

    def field(item, f):
        return tab_ref[item * _ITEM_FIELDS + f]

    def iteration(i, parity):
        p = parity
        stage_logs(field(i + 2, 2), p)
        if parity == 0:
            stage_weights(True, p)
        else:
            stage_weights(False, p)
            carry_all[field(i + 1, 3)] = carry_ref[...]
        stage_scores(field(i + 3, 5), field(i + 3, 1), field(i + 3, 4), p)
        stage_output(field(i, 0), field(i, 1), field(i, 4), parity == 1, p)
        if parity == 0:
            acc_all[field(i, 3)] = acc_ref[...]

    def body(k, c):
        for j in range(_UNROLL):
            iteration(_UNROLL * k + j, j % 2)
        return c

    lax.fori_loop(0, n_iter // _UNROLL, body, 0)

    def saturated():
        return jnp.max(carry_ref[...]) <= LOG_UNDERFLOW

    n_tiles = s // t
    first_open = _PIPELINED_BLOCKS - 1

    def finish_tile(k, c):
        slot = open_ref[k]
        grp = slot // n_tiles
        qi = slot % n_tiles
        n_blocks = qi + 2
        carry_ref[...] = carry_all[slot]
        acc_ref[...] = acc_all[slot]
        q_off = qi * t

        def more(state):
            n, done = state
            return jnp.logical_and(n < n_blocks, jnp.logical_not(done))

        def one_block(state):
            n, _ = state
            is_meta = n == n_blocks - 1
            k_off = jnp.where(is_meta, s, (qi - n) * t)
            kind = jnp.where(is_meta, _KIND_META, _KIND_PLAIN)
            stage_scores(q_off, k_off, grp, 0)
            stage_logs(kind, 1)
            stage_weights(None, 0)
            stage_output(q_off, k_off, grp, None, 1)
            return n + 1, saturated()

        lax.while_loop(more, one_block, (jnp.int32(_PIPELINED_BLOCKS), jnp.bool_(False)))
        return c

    open_slots = [grp * n_tiles + qi for grp in range(STEP_GROUPS)
                  for qi in range(first_open, n_tiles)]
    worst = carry_all[open_slots[0]]
    for slot in open_slots[1:]:
        worst = jnp.maximum(worst, carry_all[slot])

    @pl.when(jnp.max(worst) > LOG_UNDERFLOW)
    def _():
        n_open = jnp.int32(0)
        for slot in open_slots:
            open_ref[n_open] = slot
            stays = jnp.max(carry_all[slot]) > LOG_UNDERFLOW
            n_open = n_open + stays.astype(jnp.int32)
        lax.fori_loop(0, n_open, finish_tile, 0)


def _attention(qkv, meta_qkv, *, t):
    b, s, _ = qkv.shape
    d = D_MODEL
    nh = HEADS_PER_GROUP
    table, n_iter = _attention_items(s, t)
    u = _suffix_sum_matrix(t, -1.0)
    bias = _mask_bias(t)
    width = STEP_GROUPS * LANES
    col0 = d // width
    n_slots = STEP_GROUPS * (s // t)
    grid_spec = pltpu.PrefetchScalarGridSpec(
        num_scalar_prefetch=1,
        grid=(b, N_GROUPS // STEP_GROUPS),
        in_specs=[
            pl.BlockSpec((None, s, width), lambda bi, g, tab: (bi, 0, g)),
            pl.BlockSpec((None, s, width), lambda bi, g, tab: (bi, 0, col0 + g)),
            pl.BlockSpec((None, s, width), lambda bi, g, tab: (bi, 0, 2 * col0 + g)),
            pl.BlockSpec((META_ROWS, width), lambda bi, g, tab: (0, col0 + g)),
            pl.BlockSpec((META_ROWS, width), lambda bi, g, tab: (0, 2 * col0 + g)),
            pl.BlockSpec(u.shape, lambda bi, g, tab: (0, 0)),
            pl.BlockSpec(bias.shape, lambda bi, g, tab: (0, 0, 0)),
        ],
        out_specs=pl.BlockSpec((None, s, width), lambda bi, g, tab: (bi, 0, g)),
        scratch_shapes=[
            pltpu.VMEM((STEP_GROUPS * nh, LANES, s + t), BF16),
            pltpu.VMEM((STEP_GROUPS * nh, s + t, LANES), BF16),
            pltpu.VMEM((2, nh, t, t), F32),
            pltpu.VMEM((2, nh, t, t), F32),
            pltpu.VMEM((2, nh, t, t), BF16),
            pltpu.VMEM((2, nh, t, t), BF16),
            pltpu.VMEM((nh, t, LANES), F32),
            pltpu.VMEM((t, LANES), F32),
            pltpu.VMEM((n_slots, nh, t, LANES), F32),
            pltpu.VMEM((n_slots, t, LANES), F32),
            pltpu.SMEM((n_slots,), jnp.int32),
        ],
    )
    kernel = functools.partial(_attn_pipe_kernel, s=s, t=t, n_iter=n_iter)
    return pl.pallas_call(
        kernel,
        grid_spec=grid_spec,
        out_shape=jax.ShapeDtypeStruct((b, s, d), BF16),
        compiler_params=_params("arbitrary", "arbitrary"),
        name="stick_breaking_attention",
    )(jnp.asarray(table), qkv, qkv, qkv, meta_qkv, meta_qkv, u, bias)


def kernel(x, meta_tokens, conv_norm, conv_w_in, conv_w, conv_w_out, attn_norm,
           attn_w_qkv, attn_w_out, mlp_norm, mlp_w1, mlp_w2, final_norm):
    b, s, d = x.shape
    depth = mlp_norm.shape[0]
    hr = x.reshape(b * s, d)
    hm = jnp.pad(meta_tokens.astype(x.dtype), ((0, META_ROWS - N_META), (0, 0)))
    zero_hist = jnp.zeros((SUBLANES, d), F32)
    gain = lambda g: g.reshape(1, d)

    stacks = {"w_in": conv_w_in, "w_out": conv_w_out, "w_qkv": attn_w_qkv,
              "w_o": attn_w_out, "w1": mlp_w1, "w2": mlp_w2}
    is_conv = lambda i: i % 2 == 0

    def mixer_weights(i):
        return [("w_in", i // 2), ("w_out", i // 2)] if is_conv(i) else [("w_qkv", i // 2)]

    def mlp_weights(i):
        return ([] if is_conv(i) else [("w_o", i // 2)]) + [("w1", i), ("w2", i)]

    first = mixer_weights(0) + ([] if is_conv(0) else mlp_weights(0))
    bf = {key: stacks[key[0]][key[1]].astype(BF16) for key in first}

    def with_casts(call, keys):
        out, *cast = call(casts=[(stacks[name], layer) for name, layer in keys])
        bf.update(zip(keys, cast))
        return out

    for i in range(depth):
        j = i // 2
        last = i == depth - 1
        final_g = gain(final_norm) if last else None
        g_mlp = gain(mlp_norm[i])
        nxt = [] if last else mixer_weights(i + 1) + ([] if is_conv(i + 1) else mlp_weights(i + 1))
        if is_conv(i):
            g = gain(conv_norm[j])
            w_in, w_out = bf["w_in", j], bf["w_out", j]
            hm, hist = _conv_mixer(hm, g, w_in, conv_w[j], w_out, zero_hist,
                                   tm=META_ROWS, tiles_per_seq=1, hist_row=N_META)
            hr = with_casts(functools.partial(
                _conv_mixer, hr, g, w_in, conv_w[j], w_out, hist,
                tm=WIDE_ROW_TILE, tiles_per_seq=s // WIDE_ROW_TILE), mlp_weights(i))
            (hm,) = _mlp(hm, g_mlp, bf["w1", i], bf["w2", i], tm=META_ROWS)
            hr = with_casts(functools.partial(
                _mlp, hr, g_mlp, bf["w1", i], bf["w2", i], tm=ROW_TILE, final_g=final_g), nxt)
        else:
            g = gain(attn_norm[j])
            qkv_m = _qkv(hm, g, bf["w_qkv", j], tm=META_ROWS)
            qkv_r = _qkv(hr, g, bf["w_qkv", j], tm=WIDE_ROW_TILE)
            o_r = _attention(qkv_r.reshape(b, s, 3 * d), qkv_m, t=ATTN_TILE)
            if not last:
                o_m = _attention_small(qkv_m)
                (hm,) = _mlp(hm, g_mlp, bf["w1", i], bf["w2", i], tm=META_ROWS,
                             attn=o_m, w_o=bf["w_o", j])
            hr = with_casts(functools.partial(
                _mlp, hr, g_mlp, bf["w1", i], bf["w2", i], tm=ROW_TILE,
                attn=o_r.reshape(b * s, d), w_o=bf["w_o", j], final_g=final_g), nxt)
    return hr.reshape(b, s, d)
```

```python
import functools
import math

import numpy as np
import jax
import jax.numpy as jnp
from jax import lax
from jax.experimental import pallas as pl
from jax.experimental.pallas import tpu as pltpu

D_MODEL = 1024
N_META = 16
N_HEADS = 16
HEAD_DIM = D_MODEL // N_HEADS
CONV_W = 3
D_FF = 4 * D_MODEL
RMS_EPS = 1e-6

LANES = 128
SUBLANES = 8
HEADS_PER_GROUP = LANES // HEAD_DIM
N_GROUPS = N_HEADS // HEADS_PER_GROUP
META_ROWS = LANES
ROW_TILE = 512
WIDE_ROW_TILE = 1024
ATTN_TILE = 256
FF_CHUNK = 1024
VMEM_LIMIT_BYTES = 56 * 1024 * 1024
LOG2_E = math.log2(math.e)
MASK_BIAS = -1e30

BF16 = jnp.bfloat16
F32 = jnp.float32


def _dot(a, b):
    return jnp.dot(a, b, preferred_element_type=F32)


def _rmsnorm(x, g):
    return x * lax.rsqrt(jnp.mean(x * x, axis=-1, keepdims=True) + RMS_EPS) * g


def _resident(shape):
    zeros = (0,) * len(shape)
    return pl.BlockSpec(shape, lambda *_: zeros, pipeline_mode=pl.Buffered(1))


def _cast_specs(casts, n_steps):
    in_specs, out_specs, out_shapes = [], [], []
    for stack, layer in casts:
        _, rows, cols = stack.shape
        chunk = rows // n_steps
        assert chunk * n_steps == rows and chunk % (2 * SUBLANES) == 0, (rows, n_steps)
        in_specs.append(pl.BlockSpec((None, chunk, cols), lambda i, layer=layer: (layer, i, 0)))
        out_specs.append(pl.BlockSpec((chunk, cols), lambda i: (i, 0)))
        out_shapes.append(jax.ShapeDtypeStruct((rows, cols), BF16))
    return in_specs, out_specs, out_shapes


def _do_casts(src_refs, dst_refs):
    for src, dst in zip(src_refs, dst_refs, strict=True):
        dst[...] = src[...].astype(BF16)


def _params(*semantics):
    return pltpu.CompilerParams(dimension_semantics=semantics,
                                vmem_limit_bytes=VMEM_LIMIT_BYTES)


def _conv_mixer_kernel(x_ref, g_ref, w_in_ref, cw_ref, w_out_ref, hist_ref,
                       *rest, tm, tiles_per_seq, hist_row, n_casts):
    rest = list(rest)
    cast_src = [rest.pop(0) for _ in range(n_casts)]
    o_ref = rest.pop(0)
    hist_out_ref = None if hist_row is None else rest.pop(0)
    cast_dst = [rest.pop(0) for _ in range(n_casts)]
    (ubuf,) = rest
    _do_casts(cast_src, cast_dst)
    i = pl.program_id(0)
    first = (i % tiles_per_seq) == 0

    @pl.when(first)
    def _():
        ubuf[0:SUBLANES, :] = hist_ref[...]

    @pl.when(jnp.logical_not(first))
    def _():
        ubuf[0:SUBLANES, :] = ubuf[tm:tm + SUBLANES, :]

    x = x_ref[...]
    h = _rmsnorm(x, g_ref[...]).astype(BF16)
    d = D_MODEL
    gate_c = _dot(h, w_in_ref[:, d:2 * d])
    val = _dot(h, w_in_ref[:, 2 * d:3 * d])
    ubuf[SUBLANES:SUBLANES + tm, :] = gate_c * val
    if hist_row is not None:
        hist_out_ref[...] = ubuf[hist_row:hist_row + SUBLANES, :]
    cw = cw_ref[...]
    conv = ubuf[SUBLANES:SUBLANES + tm, :] * cw[CONV_W - 1:CONV_W, :]
    for k in range(CONV_W - 1):
        back = CONV_W - 1 - k
        conv = conv + ubuf[SUBLANES - back:SUBLANES - back + tm, :] * cw[k:k + 1, :]
    gate_b = _dot(h, w_in_ref[:, 0:d])
    y = (gate_b * conv).astype(BF16)
    o_ref[...] = x + _dot(y, w_out_ref[...])


def _conv_mixer(x, g, w_in, cw, w_out, hist, *, tm, tiles_per_seq, hist_row=None, casts=()):
    n, d = x.shape
    n_steps = n // tm
    kernel = functools.partial(_conv_mixer_kernel, tm=tm, tiles_per_seq=tiles_per_seq,
                               hist_row=hist_row, n_casts=len(casts))
    row = pl.BlockSpec((tm, d), lambda i: (i, 0))
    cast_in, cast_out, cast_shapes = _cast_specs(casts, n_steps)
    out_shapes = [jax.ShapeDtypeStruct((n, d), F32)]
    out_specs = [row]
    if hist_row is not None:
        out_shapes.append(jax.ShapeDtypeStruct((SUBLANES, d), F32))
        out_specs.append(pl.BlockSpec((SUBLANES, d), lambda i: (0, 0)))
    return pl.pallas_call(
        kernel,
        grid=(n_steps,),
        in_specs=[row, _resident((1, d)), _resident((d, 3 * d)), _resident((CONV_W, d)),
                  _resident((d, d)), _resident((SUBLANES, d))] + cast_in,
        out_specs=out_specs + cast_out,
        out_shape=out_shapes + cast_shapes,
        scratch_shapes=[pltpu.VMEM((tm + 2 * SUBLANES, d), F32)],
        compiler_params=_params("arbitrary"),
        name="conv_mixer",
    )(x, g, w_in, cw, w_out, hist, *[stack for stack, _ in casts])


def _mlp_kernel(*refs, with_proj, with_final_norm, n_casts):
    refs = list(refs)
    x_ref = refs.pop(0)
    if with_proj:
        a_ref, w_o_ref = refs.pop(0), refs.pop(0)
    g_ref, w1_ref, w2_ref = refs.pop(0), refs.pop(0), refs.pop(0)
    if with_final_norm:
        gf_ref = refs.pop(0)
    cast_src = [refs.pop(0) for _ in range(n_casts)]
    o_ref = refs.pop(0)
    _do_casts(cast_src, refs)

    x = x_ref[...]
    if with_proj:
        x = x + _dot(a_ref[...], w_o_ref[...])
    h = _rmsnorm(x, g_ref[...]).astype(BF16)
    acc = x
    for c in range(D_FF // FF_CHUNK):
        cols = slice(c * FF_CHUNK, (c + 1) * FF_CHUNK)
        a = jnp.maximum(_dot(h, w1_ref[:, cols]), 0.0)
        acc = acc + _dot((a * a).astype(BF16), w2_ref[cols, :])
    if with_final_norm:
        acc = _rmsnorm(acc, gf_ref[...])
    o_ref[...] = acc


def _mlp(x, g, w1, w2, *, tm, attn=None, w_o=None, final_g=None, casts=()):
    n, d = x.shape
    n_steps = n // tm
    with_proj = attn is not None
    with_final_norm = final_g is not None
    row = pl.BlockSpec((tm, d), lambda i: (i, 0))
    args, specs = [x], [row]
    if with_proj:
        args += [attn, w_o]
        specs += [row, _resident((d, d))]
    args += [g, w1, w2]
    specs += [_resident((1, d)), _resident((d, D_FF)), _resident((D_FF, d))]
    if with_final_norm:
        args.append(final_g)
        specs.append(_resident((1, d)))
    cast_in, cast_out, cast_shapes = _cast_specs(casts, n_steps)
    kernel = functools.partial(_mlp_kernel, with_proj=with_proj,
                               with_final_norm=with_final_norm, n_casts=len(casts))
    return pl.pallas_call(
        kernel,
        grid=(n_steps,),
        in_specs=specs + cast_in,
        out_specs=[row] + cast_out,
        out_shape=[jax.ShapeDtypeStruct((n, d), F32)] + cast_shapes,
        compiler_params=_params("arbitrary"),
        name="mlp",
    )(*args, *[stack for stack, _ in casts])


def _qkv_kernel(x_ref, g_ref, w_qkv_ref, qkv_ref, kt_ref):
    d = D_MODEL
    h = _rmsnorm(x_ref[...], g_ref[...]).astype(BF16)
    scale = 1.0 / math.sqrt(HEAD_DIM)
    qkv_ref[:, 0:d] = (_dot(h, w_qkv_ref[:, 0:d]) * scale).astype(BF16)
    qkv_ref[:, d:3 * d] = _dot(h, w_qkv_ref[:, d:3 * d]).astype(BF16)
    for c in range(0, d, LANES):
        kt_ref[c:c + LANES, :] = qkv_ref[:, d + c:d + c + LANES].T


def _qkv(x, g, w_qkv, *, tm, seq):
    n, d = x.shape
    tiles_per_seq = seq // tm
    return pl.pallas_call(
        _qkv_kernel,
        grid=(n // tm,),
        in_specs=[pl.BlockSpec((tm, d), lambda i: (i, 0)), _resident((1, d)),
                  _resident((d, 3 * d))],
        out_specs=[pl.BlockSpec((tm, 3 * d), lambda i: (i, 0)),
                   pl.BlockSpec((None, d, tm),
                                lambda i: (i // tiles_per_seq, 0, i % tiles_per_seq))],
        out_shape=[jax.ShapeDtypeStruct((n, 3 * d), BF16),
                   jax.ShapeDtypeStruct((n // seq, d, seq), BF16)],
        compiler_params=_params("arbitrary"),
        name="qkv_proj",
    )(x, g, w_qkv)


def _suffix_sum_matrix(t, sign):
    r = jnp.arange(t)[:, None]
    c = jnp.arange(t + LANES)[None, :]
    return jnp.where((c >= t) | (r > c), sign, 0.0).astype(BF16)


def _head_of_lane(shape):
    return lax.broadcasted_iota(jnp.int32, shape, len(shape) - 1) // HEAD_DIM


def _attn_small_kernel(q_ref, k_ref, v_ref, u_ref, o_ref, *, t):
    q = q_ref[...]
    k = k_ref[...]
    head_of_lane = _head_of_lane((t, LANES))
    row_i = lax.broadcasted_iota(jnp.int32, (t, t), 0)
    col_i = lax.broadcasted_iota(jnp.int32, (t, t), 1)
    causal = col_i < row_i
    out = None
    for hh in range(HEADS_PER_GROUP):
        qh = jnp.where(head_of_lane == hh, q, jnp.zeros_like(q))
        z = lax.dot_general(qh, k, (((1,), (1,)), ((), ())), preferred_element_type=F32)
        soft = jnp.log(1.0 + jnp.exp(-jnp.abs(z)))
        log_beta = jnp.minimum(z, 0.0) - soft
        log_1m = jnp.where(causal, log_beta - z, 0.0)
        tail = _dot(log_1m.astype(BF16), u_ref[:, 0:t])
        w = jnp.where(causal, jnp.exp(log_beta + tail), 0.0)
        oh = _dot(w.astype(BF16), v_ref[...])
        out = oh if out is None else jnp.where(head_of_lane == hh, oh, out)
    o_ref[...] = out.astype(BF16)


def _attention_small(qkv):
    t = qkv.shape[0]
    d = D_MODEL
    u = _suffix_sum_matrix(t, 1.0)
    col0 = d // LANES
    return pl.pallas_call(
        functools.partial(_attn_small_kernel, t=t),
        grid=(N_GROUPS,),
        in_specs=[pl.BlockSpec((t, LANES), lambda g: (0, g)),
                  pl.BlockSpec((t, LANES), lambda g: (0, col0 + g)),
                  pl.BlockSpec((t, LANES), lambda g: (0, 2 * col0 + g)),
                  _resident(u.shape)],
        out_specs=pl.BlockSpec((t, LANES), lambda g: (0, g)),
        out_shape=jax.ShapeDtypeStruct((t, d), BF16),
        compiler_params=_params("arbitrary"),
        name="attention_meta",
    )(qkv, qkv, qkv, u)


_KIND_PLAIN, _KIND_DIAG, _KIND_META = 0, 1, 2
_PIPE_DEPTH = 4
_UNROLL = 34
_ITEM_FIELDS = 6
_PIPELINED_BLOCKS = 2
assert _PIPELINED_BLOCKS == 2 and _PIPE_DEPTH % 2 == 0 and _UNROLL % 2 == 0
STEP_GROUPS = 4
LOG_UNDERFLOW = -105.0


def _tile_blocks(qi, s, t):
    blocks = [(qi * t, _KIND_DIAG)]
    blocks += [(j * t, _KIND_PLAIN) for j in range(qi - 1, -1, -1)]
    blocks.append((s, _KIND_META))
    return blocks


def _attention_items(s, t):
    n_tiles = s // t
    items = []
    for grp in range(STEP_GROUPS):
        for qi in range(n_tiles):
            for k_off, kind in _tile_blocks(qi, s, t)[:_PIPELINED_BLOCKS]:
                items.append([qi * t, k_off, kind, grp * n_tiles + qi, grp, qi * t])
    lead = _PIPE_DEPTH - 1
    n_iter = len(items) + lead
    n_iter += -n_iter % _UNROLL
    first_pair, last_pair = items[:2], items[-2:]
    table = (first_pair * 2)[-lead:] + items
    while len(table) < n_iter + lead:
        table.append(last_pair[(len(table) - lead) % 2])
    return np.asarray(table, np.int32).reshape(-1), n_iter


def _mask_bias(t):
    r = np.arange(t)[:, None]
    c = np.arange(t)[None, :]
    bias = np.zeros((3, t, t), np.float32)
    bias[_KIND_DIAG] = np.where(c < r, 0.0, MASK_BIAS)
    bias[_KIND_META] = np.where(c < N_META, 0.0, MASK_BIAS) + 0.0 * r
    return jnp.asarray(bias)


def _attn_pipe_kernel(tab_ref, q_ref, kt_ref, v_ref, ktm_ref, vm_ref, u_ref, bias_ref,
                      o_ref, kt_all, v_all, zbuf, lbuf, pbuf, wbuf, carry_ref,
                      acc_ref, carry_all, acc_all, open_ref, *, s, t, n_iter):
    nh = HEADS_PER_GROUP
    first_step = jnp.logical_and(pl.program_id(0) == 0, pl.program_id(1) == 0)

    @pl.when(first_step)
    def _():
        for ref in (zbuf, lbuf, pbuf, wbuf, carry_ref, acc_ref):
            ref[...] = jnp.zeros_like(ref)
        kt_all[...] = jnp.zeros_like(kt_all)
        v_all[:, s + META_ROWS:s + t, :] = jnp.zeros(
            (STEP_GROUPS * nh, t - META_ROWS, LANES), BF16)

    head_of_lane = _head_of_lane((1, LANES))
    for grp in range(STEP_GROUPS):
        cols = slice(grp * LANES, (grp + 1) * LANES)
        v = v_ref[:, cols]
        vm = vm_ref[:, cols]
        for hh in range(nh):
            mine = head_of_lane == hh
            v_all[grp * nh + hh, 0:s, :] = jnp.where(mine, v, jnp.zeros_like(v))
            v_all[grp * nh + hh, s:s + META_ROWS, :] = jnp.where(mine, vm, jnp.zeros_like(vm))
        for hh in range(nh):
            rows = slice(hh * HEAD_DIM, (hh + 1) * HEAD_DIM)
            src = slice(grp * LANES + hh * HEAD_DIM, grp * LANES + (hh + 1) * HEAD_DIM)
            kt_all[grp * nh + hh, rows, 0:s] = kt_ref[src, :]
            kt_all[grp * nh + hh, rows, s:s + META_ROWS] = ktm_ref[src, :]

    def stage_scores(q_off, k_off, grp, p):
        q_blk = q_ref[pl.ds(pl.multiple_of(q_off, t), t),
                      pl.ds(pl.multiple_of(grp * LANES, LANES), LANES)]
        for hh in range(nh):
            zbuf[p, hh] = _dot(q_blk, kt_all[grp * nh + hh, :, pl.ds(pl.multiple_of(k_off, t), t)])

    def stage_logs(kind, p):
        bias = bias_ref[kind]
        for hh in range(nh):
            z = zbuf[1 - p, hh] + bias
            soft = jnp.log(1.0 + jnp.exp2(jnp.abs(z) * (-LOG2_E)))
            neg_log_1m = jnp.maximum(z, 0.0) + soft
            lbuf[p, hh] = z - neg_log_1m
            pbuf[p, hh] = neg_log_1m.astype(BF16)

    def stage_weights(first, p):
        for hh in range(nh):
            sums = _dot(pbuf[1 - p, hh], u_ref[...])
            tail, row_sum = sums[:, 0:t], sums[:, t:t + LANES]
            if first:
                carry_ref[hh] = row_sum
            else:
                carry = carry_ref[hh]
                tail = tail + jnp.concatenate([carry] * (t // LANES), axis=1)
                carry_ref[hh] = carry + row_sum
            wbuf[p, hh] = jnp.exp(lbuf[1 - p, hh] + tail).astype(BF16)

    def stage_output(q_off, k_off, grp, first, p):
        k_off = pl.multiple_of(k_off, t)
        w_cat = jnp.concatenate([wbuf[1 - p, hh] for hh in range(nh)], axis=1)
        v_cat = jnp.concatenate([v_all[grp * nh + hh, pl.ds(k_off, t), :]
                                 for hh in range(nh)], axis=0)
        acc = _dot(w_cat, v_cat)
        if not first:
            acc = acc_ref[...] + acc
            o_ref[pl.ds(pl.multiple_of(q_off, t), t),
                  pl.ds(pl.multiple_of(grp * LANES, LANES), LANES)] = acc.astype(BF16)
        acc_ref[...] = acc

    def field(item, f):
        return tab_ref[item * _ITEM_FIELDS + f]

    def iteration(i, parity):
        p = parity
        stage_logs(field(i + 2, 2), p)
        if parity == 0:
            stage_weights(True, p)
        else:
            stage_weights(False, p)
            carry_all[field(i + 1, 3)] = carry_ref[...]
        stage_scores(field(i + 3, 5), field(i + 3, 1), field(i + 3, 4), p)
        stage_output(field(i, 0), field(i, 1), field(i, 4), parity == 1, p)
        if parity == 0:
            acc_all[field(i, 3)] = acc_ref[...]

    def body(k, c):
        for j in range(_UNROLL):
            iteration(_UNROLL * k + j, j % 2)
        return c

    lax.fori_loop(0, n_iter // _UNROLL, body, 0)

    def saturated():
        return jnp.max(carry_ref[...]) <= LOG_UNDERFLOW

    n_tiles = s // t
    first_open = _PIPELINED_BLOCKS - 1

    def finish_tile(k, c):
        slot = open_ref[k]
        grp = slot // n_tiles
        qi = slot % n_tiles
        n_blocks = qi + 2
        carry_ref[...] = carry_all[slot]
        acc_ref[...] = acc_all[slot]
        q_off = qi * t

        def more(state):
            n, done = state
            return jnp.logical_and(n < n_blocks, jnp.logical_not(done))

        def one_block(state):
            n, _ = state
            is_meta = n == n_blocks - 1
            k_off = jnp.where(is_meta, s, (qi - n) * t)
            kind = jnp.where(is_meta, _KIND_META, _KIND_PLAIN)
            stage_scores(q_off, k_off, grp, 0)
            stage_logs(kind, 1)
            stage_weights(None, 0)
            stage_output(q_off, k_off, grp, None, 1)
            return n + 1, saturated()

        lax.while_loop(more, one_block, (jnp.int32(_PIPELINED_BLOCKS), jnp.bool_(False)))
        return c

    open_slots = [grp * n_tiles + qi for grp in range(STEP_GROUPS)
                  for qi in range(first_open, n_tiles)]
    worst = carry_all[open_slots[0]]
    for slot in open_slots[1:]:
        worst = jnp.maximum(worst, carry_all[slot])

    @pl.when(jnp.max(worst) > LOG_UNDERFLOW)
    def _():
        n_open = jnp.int32(0)
        for slot in open_slots:
            open_ref[n_open] = slot
            stays = jnp.max(carry_all[slot]) > LOG_UNDERFLOW
            n_open = n_open + stays.astype(jnp.int32)
        lax.fori_loop(0, n_open, finish_tile, 0)


def _attention(qkv, kt, meta_qkv, meta_kt, *, t):
    b, s, _ = qkv.shape
    d = D_MODEL
    nh = HEADS_PER_GROUP
    table, n_iter = _attention_items(s, t)
    u = _suffix_sum_matrix(t, -1.0)
    bias = _mask_bias(t)
    width = STEP_GROUPS * LANES
    col0 = d // width
    n_slots = STEP_GROUPS * (s // t)
    grid_spec = pltpu.PrefetchScalarGridSpec(
        num_scalar_prefetch=1,
        grid=(b, N_GROUPS // STEP_GROUPS),
        in_specs=[
            pl.BlockSpec((None, s, width), lambda bi, g, tab: (bi, 0, g)),
            pl.BlockSpec((None, width, s), lambda bi, g, tab: (bi, g, 0)),
            pl.BlockSpec((None, s, width), lambda bi, g, tab: (bi, 0, 2 * col0 + g)),
            pl.BlockSpec((width, META_ROWS), lambda bi, g, tab: (g, 0)),
            pl.BlockSpec((META_ROWS, width), lambda bi, g, tab: (0, 2 * col0 + g)),
            pl.BlockSpec(u.shape, lambda bi, g, tab: (0, 0)),
            pl.BlockSpec(bias.shape, lambda bi, g, tab: (0, 0, 0)),
        ],
        out_specs=pl.BlockSpec((None, s, width), lambda bi, g, tab: (bi, 0, g)),
        scratch_shapes=[
            pltpu.VMEM((STEP_GROUPS * nh, LANES, s + t), BF16),
            pltpu.VMEM((STEP_GROUPS * nh, s + t, LANES), BF16),
            pltpu.VMEM((2, nh, t, t), F32),
            pltpu.VMEM((2, nh, t, t), F32),
            pltpu.VMEM((2, nh, t, t), BF16),
            pltpu.VMEM((2, nh, t, t), BF16),
            pltpu.VMEM((nh, t, LANES), F32),
            pltpu.VMEM((t, LANES), F32),
            pltpu.VMEM((n_slots, nh, t, LANES), F32),
            pltpu.VMEM((n_slots, t, LANES), F32),
            pltpu.SMEM((n_slots,), jnp.int32),
        ],
    )
    kernel = functools.partial(_attn_pipe_kernel, s=s, t=t, n_iter=n_iter)
    return pl.pallas_call(
        kernel,
        grid_spec=grid_spec,
        out_shape=jax.ShapeDtypeStruct((b, s, d), BF16),
        compiler_params=_params("arbitrary", "arbitrary"),
        name="stick_breaking_attention",
    )(jnp.asarray(table), qkv, kt, qkv, meta_kt, meta_qkv, u, bias)


def kernel(x, meta_tokens, conv_norm, conv_w_in, conv_w, conv_w_out, attn_norm,
           attn_w_qkv, attn_w_out, mlp_norm, mlp_w1, mlp_w2, final_norm):
    b, s, d = x.shape
    depth = mlp_norm.shape[0]
    hr = x.reshape(b * s, d)
    hm = jnp.pad(meta_tokens.astype(x.dtype), ((0, META_ROWS - N_META), (0, 0)))
    zero_hist = jnp.zeros((SUBLANES, d), F32)
    gain = lambda g: g.reshape(1, d)

    stacks = {"w_in": conv_w_in, "w_out": conv_w_out, "w_qkv": attn_w_qkv,
              "w_o": attn_w_out, "w1": mlp_w1, "w2": mlp_w2}
    is_conv = lambda i: i % 2 == 0

    def mixer_weights(i):
        return [("w_in", i // 2), ("w_out", i // 2)] if is_conv(i) else [("w_qkv", i // 2)]

    def mlp_weights(i):
        return ([] if is_conv(i) else [("w_o", i // 2)]) + [("w1", i), ("w2", i)]

    first = mixer_weights(0) + ([] if is_conv(0) else mlp_weights(0))
    bf = {key: stacks[key[0]][key[1]].astype(BF16) for key in first}

    def with_casts(call, keys):
        out, *cast = call(casts=[(stacks[name], layer) for name, layer in keys])
        bf.update(zip(keys, cast))
        return out

    for i in range(depth):
        j = i // 2
        last = i == depth - 1
        final_g = gain(final_norm) if last else None
        g_mlp = gain(mlp_norm[i])
        nxt = [] if last else mixer_weights(i + 1) + ([] if is_conv(i + 1) else mlp_weights(i + 1))
        if is_conv(i):
            g = gain(conv_norm[j])
            w_in, w_out = bf["w_in", j], bf["w_out", j]
            hm, hist = _conv_mixer(hm, g, w_in, conv_w[j], w_out, zero_hist,
                                   tm=META_ROWS, tiles_per_seq=1, hist_row=N_META)
            hr = with_casts(functools.partial(
                _conv_mixer, hr, g, w_in, conv_w[j], w_out, hist,
                tm=WIDE_ROW_TILE, tiles_per_seq=s // WIDE_ROW_TILE), mlp_weights(i))
            (hm,) = _mlp(hm, g_mlp, bf["w1", i], bf["w2", i], tm=META_ROWS)
            hr = with_casts(functools.partial(
                _mlp, hr, g_mlp, bf["w1", i], bf["w2", i], tm=ROW_TILE, final_g=final_g), nxt)
        else:
            g = gain(attn_norm[j])
            qkv_m, kt_m = _qkv(hm, g, bf["w_qkv", j], tm=META_ROWS, seq=META_ROWS)
            qkv_r, kt_r = _qkv(hr, g, bf["w_qkv", j], tm=WIDE_ROW_TILE, seq=s)
            o_r = _attention(qkv_r.reshape(b, s, 3 * d), kt_r, qkv_m, kt_m[0], t=ATTN_TILE)
            if not last:
                o_m = _attention_small(qkv_m)
                (hm,) = _mlp(hm, g_mlp, bf["w1", i], bf["w2", i], tm=META_ROWS,
                             attn=o_m, w_o=bf["w_o", j])
            hr = with_casts(functools.partial(
                _mlp, hr, g_mlp, bf["w1", i], bf["w2", i], tm=ROW_TILE,
                attn=o_r.reshape(b * s, d), w_o=bf["w_o", j], final_g=final_g), nxt)
    return hr.reshape(b, s, d)
```

```python
import functools
import math

import numpy as np
import jax
import jax.numpy as jnp
from jax import lax
from jax.experimental import pallas as pl
from jax.experimental.pallas import tpu as pltpu

D_MODEL = 1024
N_META = 16
N_HEADS = 16
HEAD_DIM = D_MODEL // N_HEADS
CONV_W = 3
D_FF = 4 * D_MODEL
RMS_EPS = 1e-6

LANES = 128
SUBLANES = 8
HEADS_PER_GROUP = LANES // HEAD_DIM
N_GROUPS = N_HEADS // HEADS_PER_GROUP
META_ROWS = LANES
ROW_TILE = 512
WIDE_ROW_TILE = 1024
ATTN_TILE = 256
FF_CHUNK = 1024
VMEM_LIMIT_BYTES = 56 * 1024 * 1024
LOG2_E = math.log2(math.e)
MASK_BIAS = -1e30

BF16 = jnp.bfloat16
F32 = jnp.float32


def _dot(a, b):
    return jnp.dot(a, b, preferred_element_type=F32)


def _rmsnorm(x, g):
    return x * lax.rsqrt(jnp.mean(x * x, axis=-1, keepdims=True) + RMS_EPS) * g


def _resident(shape):
    zeros = (0,) * len(shape)
    return pl.BlockSpec(shape, lambda *_: zeros, pipeline_mode=pl.Buffered(1))


def _cast_specs(casts, n_steps):
    in_specs, out_specs, out_shapes = [], [], []
    for stack, layer in casts:
        _, rows, cols = stack.shape
        chunk = rows // n_steps
        assert chunk * n_steps == rows and chunk % (2 * SUBLANES) == 0, (rows, n_steps)
        in_specs.append(pl.BlockSpec((None, chunk, cols), lambda i, layer=layer: (layer, i, 0)))
        out_specs.append(pl.BlockSpec((chunk, cols), lambda i: (i, 0)))
        out_shapes.append(jax.ShapeDtypeStruct((rows, cols), BF16))
    return in_specs, out_specs, out_shapes


def _do_casts(src_refs, dst_refs):
    for src, dst in zip(src_refs, dst_refs, strict=True):
        dst[...] = src[...].astype(BF16)


def _params(*semantics):
    return pltpu.CompilerParams(dimension_semantics=semantics,
                                vmem_limit_bytes=VMEM_LIMIT_BYTES)


def _conv_mixer_kernel(x_ref, g_ref, w_in_ref, cw_ref, w_out_ref, hist_ref,
                       *rest, tm, tiles_per_seq, hist_row, n_casts):
    rest = list(rest)
    cast_src = [rest.pop(0) for _ in range(n_casts)]
    o_ref = rest.pop(0)
    hist_out_ref = None if hist_row is None else rest.pop(0)
    cast_dst = [rest.pop(0) for _ in range(n_casts)]
    (ubuf,) = rest
    _do_casts(cast_src, cast_dst)
    i = pl.program_id(0)
    first = (i % tiles_per_seq) == 0

    @pl.when(first)
    def _():
        ubuf[0:SUBLANES, :] = hist_ref[...]

    @pl.when(jnp.logical_not(first))
    def _():
        ubuf[0:SUBLANES, :] = ubuf[tm:tm + SUBLANES, :]

    x = x_ref[...]
    h = _rmsnorm(x, g_ref[...]).astype(BF16)
    d = D_MODEL
    gate_c = _dot(h, w_in_ref[:, d:2 * d])
    val = _dot(h, w_in_ref[:, 2 * d:3 * d])
    ubuf[SUBLANES:SUBLANES + tm, :] = gate_c * val
    if hist_row is not None:
        hist_out_ref[...] = ubuf[hist_row:hist_row + SUBLANES, :]
    cw = cw_ref[...]
    conv = ubuf[SUBLANES:SUBLANES + tm, :] * cw[CONV_W - 1:CONV_W, :]
    for k in range(CONV_W - 1):
        back = CONV_W - 1 - k
        conv = conv + ubuf[SUBLANES - back:SUBLANES - back + tm, :] * cw[k:k + 1, :]
    gate_b = _dot(h, w_in_ref[:, 0:d])
    y = (gate_b * conv).astype(BF16)
    o_ref[...] = x + _dot(y, w_out_ref[...])


def _conv_mixer(x, g, w_in, cw, w_out, hist, *, tm, tiles_per_seq, hist_row=None, casts=()):
    n, d = x.shape
    n_steps = n // tm
    kernel = functools.partial(_conv_mixer_kernel, tm=tm, tiles_per_seq=tiles_per_seq,
                               hist_row=hist_row, n_casts=len(casts))
    row = pl.BlockSpec((tm, d), lambda i: (i, 0))
    cast_in, cast_out, cast_shapes = _cast_specs(casts, n_steps)
    out_shapes = [jax.ShapeDtypeStruct((n, d), F32)]
    out_specs = [row]
    if hist_row is not None:
        out_shapes.append(jax.ShapeDtypeStruct((SUBLANES, d), F32))
        out_specs.append(pl.BlockSpec((SUBLANES, d), lambda i: (0, 0)))
    return pl.pallas_call(
        kernel,
        grid=(n_steps,),
        in_specs=[row, _resident((1, d)), _resident((d, 3 * d)), _resident((CONV_W, d)),
                  _resident((d, d)), _resident((SUBLANES, d))] + cast_in,
        out_specs=out_specs + cast_out,
        out_shape=out_shapes + cast_shapes,
        scratch_shapes=[pltpu.VMEM((tm + 2 * SUBLANES, d), F32)],
        compiler_params=_params("arbitrary"),
        name="conv_mixer",
    )(x, g, w_in, cw, w_out, hist, *[stack for stack, _ in casts])


def _mlp_kernel(*refs, with_proj, with_final_norm, n_casts):
    refs = list(refs)
    x_ref = refs.pop(0)
    if with_proj:
        a_ref, w_o_ref = refs.pop(0), refs.pop(0)
    g_ref, w1_ref, w2_ref = refs.pop(0), refs.pop(0), refs.pop(0)
    if with_final_norm:
        gf_ref = refs.pop(0)
    cast_src = [refs.pop(0) for _ in range(n_casts)]
    o_ref = refs.pop(0)
    _do_casts(cast_src, refs)

    x = x_ref[...]
    if with_proj:
        x = x + _dot(a_ref[...], w_o_ref[...])
    h = _rmsnorm(x, g_ref[...]).astype(BF16)
    acc = x
    for c in range(D_FF // FF_CHUNK):
        cols = slice(c * FF_CHUNK, (c + 1) * FF_CHUNK)
        a = jnp.maximum(_dot(h, w1_ref[:, cols]), 0.0)
        acc = acc + _dot((a * a).astype(BF16), w2_ref[cols, :])
    if with_final_norm:
        acc = _rmsnorm(acc, gf_ref[...])
    o_ref[...] = acc


def _mlp(x, g, w1, w2, *, tm, attn=None, w_o=None, final_g=None, casts=()):
    n, d = x.shape
    n_steps = n // tm
    with_proj = attn is not None
    with_final_norm = final_g is not None
    row = pl.BlockSpec((tm, d), lambda i: (i, 0))
    args, specs = [x], [row]
    if with_proj:
        args += [attn, w_o]
        specs += [row, _resident((d, d))]
    args += [g, w1, w2]
    specs += [_resident((1, d)), _resident((d, D_FF)), _resident((D_FF, d))]
    if with_final_norm:
        args.append(final_g)
        specs.append(_resident((1, d)))
    cast_in, cast_out, cast_shapes = _cast_specs(casts, n_steps)
    kernel = functools.partial(_mlp_kernel, with_proj=with_proj,
                               with_final_norm=with_final_norm, n_casts=len(casts))
    return pl.pallas_call(
        kernel,
        grid=(n_steps,),
        in_specs=specs + cast_in,
        out_specs=[row] + cast_out,
        out_shape=[jax.ShapeDtypeStruct((n, d), F32)] + cast_shapes,
        compiler_params=_params("arbitrary"),
        name="mlp",
    )(*args, *[stack for stack, _ in casts])


def _qkv_kernel(x_ref, g_ref, w_qkv_ref, qkv_ref, kt_ref):
    d = D_MODEL
    h = _rmsnorm(x_ref[...], g_ref[...]).astype(BF16)
    scale = 1.0 / math.sqrt(HEAD_DIM)
    qkv_ref[:, 0:d] = (_dot(h, w_qkv_ref[:, 0:d]) * scale).astype(BF16)
    qkv_ref[:, d:3 * d] = _dot(h, w_qkv_ref[:, d:3 * d]).astype(BF16)
    for c in range(0, d, LANES):
        kt_ref[c:c + LANES, :] = qkv_ref[:, d + c:d + c + LANES].T


def _qkv(x, g, w_qkv, *, tm, seq):
    n, d = x.shape
    tiles_per_seq = seq // tm
    return pl.pallas_call(
        _qkv_kernel,
        grid=(n // tm,),
        in_specs=[pl.BlockSpec((tm, d), lambda i: (i, 0)), _resident((1, d)),
                  _resident((d, 3 * d))],
        out_specs=[pl.BlockSpec((tm, 3 * d), lambda i: (i, 0)),
                   pl.BlockSpec((None, d, tm),
                                lambda i: (i // tiles_per_seq, 0, i % tiles_per_seq))],
        out_shape=[jax.ShapeDtypeStruct((n, 3 * d), BF16),
                   jax.ShapeDtypeStruct((n // seq, d, seq), BF16)],
        compiler_params=_params("arbitrary"),
        name="qkv_proj",
    )(x, g, w_qkv)


def _suffix_sum_matrix(t, sign):
    r = jnp.arange(t)[:, None]
    c = jnp.arange(t + LANES)[None, :]
    return jnp.where((c >= t) | (r > c), sign, 0.0).astype(BF16)


def _head_of_lane(shape):
    return lax.broadcasted_iota(jnp.int32, shape, len(shape) - 1) // HEAD_DIM


def _attn_small_kernel(q_ref, k_ref, v_ref, u_ref, o_ref, *, t):
    q = q_ref[...]
    k = k_ref[...]
    head_of_lane = _head_of_lane((t, LANES))
    row_i = lax.broadcasted_iota(jnp.int32, (t, t), 0)
    col_i = lax.broadcasted_iota(jnp.int32, (t, t), 1)
    causal = col_i < row_i
    out = None
    for hh in range(HEADS_PER_GROUP):
        qh = jnp.where(head_of_lane == hh, q, jnp.zeros_like(q))
        z = lax.dot_general(qh, k, (((1,), (1,)), ((), ())), preferred_element_type=F32)
        soft = jnp.log(1.0 + jnp.exp(-jnp.abs(z)))
        log_beta = jnp.minimum(z, 0.0) - soft
        log_1m = jnp.where(causal, log_beta - z, 0.0)
        tail = _dot(log_1m.astype(BF16), u_ref[:, 0:t])
        w = jnp.where(causal, jnp.exp(log_beta + tail), 0.0)
        oh = _dot(w.astype(BF16), v_ref[...])
        out = oh if out is None else jnp.where(head_of_lane == hh, oh, out)
    o_ref[...] = out.astype(BF16)


def _attention_small(qkv):
    t = qkv.shape[0]
    d = D_MODEL
    u = _suffix_sum_matrix(t, 1.0)
    col0 = d // LANES
    return pl.pallas_call(
        functools.partial(_attn_small_kernel, t=t),
        grid=(N_GROUPS,),
        in_specs=[pl.BlockSpec((t, LANES), lambda g: (0, g)),
                  pl.BlockSpec((t, LANES), lambda g: (0, col0 + g)),
                  pl.BlockSpec((t, LANES), lambda g: (0, 2 * col0 + g)),
                  _resident(u.shape)],
        out_specs=pl.BlockSpec((t, LANES), lambda g: (0, g)),
        out_shape=jax.ShapeDtypeStruct((t, d), BF16),
        compiler_params=_params("arbitrary"),
        name="attention_meta",
    )(qkv, qkv, qkv, u)


_KIND_PLAIN, _KIND_DIAG, _KIND_META = 0, 1, 2
_PIPE_DEPTH = 4
_UNROLL = 34
_ITEM_FIELDS = 5
_PIPELINED_BLOCKS = 2
assert _PIPELINED_BLOCKS == 2 and _PIPE_DEPTH % 2 == 0 and _UNROLL % 2 == 0
STEP_GROUPS = 4
LOG_UNDERFLOW = -105.0


def _tile_blocks(qi, s, t):
    blocks = [(qi * t, _KIND_DIAG)]
    blocks += [(j * t, _KIND_PLAIN) for j in range(qi - 1, -1, -1)]
    blocks.append((s, _KIND_META))
    return blocks


def _attention_items(s, t):
    n_tiles = s // t
    items = []
    for grp in range(STEP_GROUPS):
        for qi in range(n_tiles):
            for k_off, kind in _tile_blocks(qi, s, t)[:_PIPELINED_BLOCKS]:
                items.append([qi * t, k_off, kind, grp * n_tiles + qi, grp])
    lead = _PIPE_DEPTH - 1
    n_iter = len(items) + lead
    n_iter += -n_iter % _UNROLL
    first_pair, last_pair = items[:2], items[-2:]
    table = (first_pair * 2)[-lead:] + items
    while len(table) < n_iter + lead:
        table.append(last_pair[(len(table) - lead) % 2])
    return np.asarray(table, np.int32).reshape(-1), n_iter


def _mask_bias(t):
    r = np.arange(t)[:, None]
    c = np.arange(t)[None, :]
    bias = np.zeros((3, t, t), np.float32)
    bias[_KIND_DIAG] = np.where(c < r, 0.0, MASK_BIAS)
    bias[_KIND_META] = np.where(c < N_META, 0.0, MASK_BIAS) + 0.0 * r
    return jnp.asarray(bias)


def _attn_pipe_kernel(tab_ref, q_ref, kt_ref, v_ref, ktm_ref, vm_ref, u_ref, bias_ref,
                      o_ref, kt_all, v_all, zbuf, lbuf, pbuf, wbuf, carry_ref,
                      acc_ref, carry_all, acc_all, open_ref, *, s, t, n_iter):
    nh = HEADS_PER_GROUP
    first_step = jnp.logical_and(pl.program_id(0) == 0, pl.program_id(1) == 0)

    @pl.when(first_step)
    def _():
        for ref in (zbuf, lbuf, pbuf, wbuf, carry_ref, acc_ref):
            ref[...] = jnp.zeros_like(ref)
        kt_all[...] = jnp.zeros_like(kt_all)
        v_all[:, s + META_ROWS:s + t, :] = jnp.zeros(
            (STEP_GROUPS * nh, t - META_ROWS, LANES), BF16)

    head_of_lane = _head_of_lane((1, LANES))
    for grp in range(STEP_GROUPS):
        cols = slice(grp * LANES, (grp + 1) * LANES)
        v = v_ref[:, cols]
        vm = vm_ref[:, cols]
        for hh in range(nh):
            mine = head_of_lane == hh
            v_all[grp * nh + hh, 0:s, :] = jnp.where(mine, v, jnp.zeros_like(v))
            v_all[grp * nh + hh, s:s + META_ROWS, :] = jnp.where(mine, vm, jnp.zeros_like(vm))
        for hh in range(nh):
            rows = slice(hh * HEAD_DIM, (hh + 1) * HEAD_DIM)
            src = slice(grp * LANES + hh * HEAD_DIM, grp * LANES + (hh + 1) * HEAD_DIM)
            kt_all[grp * nh + hh, rows, 0:s] = kt_ref[src, :]
            kt_all[grp * nh + hh, rows, s:s + META_ROWS] = ktm_ref[src, :]

    def stage_scores(q_off, k_off, grp, p):
        q_blk = q_ref[pl.ds(pl.multiple_of(q_off, t), t),
                      pl.ds(pl.multiple_of(grp * LANES, LANES), LANES)]
        for hh in range(nh):
            zbuf[p, hh] = _dot(q_blk, kt_all[grp * nh + hh, :, pl.ds(pl.multiple_of(k_off, t), t)])

    def stage_logs(kind, p):
        bias = bias_ref[kind]
        for hh in range(nh):
            z = zbuf[1 - p, hh] + bias
            soft = jnp.log(1.0 + jnp.exp2(jnp.abs(z) * (-LOG2_E)))
            neg_log_1m = jnp.maximum(z, 0.0) + soft
            lbuf[p, hh] = z - neg_log_1m
            pbuf[p, hh] = neg_log_1m.astype(BF16)

    def stage_weights(first, p):
        for hh in range(nh):
            sums = _dot(pbuf[1 - p, hh], u_ref[...])
            tail, row_sum = sums[:, 0:t], sums[:, t:t + LANES]
            if first:
                carry_ref[hh] = row_sum
            else:
                carry = carry_ref[hh]
                tail = tail + jnp.concatenate([carry] * (t // LANES), axis=1)
                carry_ref[hh] = carry + row_sum
            wbuf[p, hh] = jnp.exp(lbuf[1 - p, hh] + tail).astype(BF16)

    def stage_output(q_off, k_off, grp, first, p):
        k_off = pl.multiple_of(k_off, t)
        w_cat = jnp.concatenate([wbuf[1 - p, hh] for hh in range(nh)], axis=1)
        v_cat = jnp.concatenate([v_all[grp * nh + hh, pl.ds(k_off, t), :]
                                 for hh in range(nh)], axis=0)
        acc = _dot(w_cat, v_cat)
        if not first:
            acc = acc_ref[...] + acc
            o_ref[pl.ds(pl.multiple_of(q_off, t), t),
                  pl.ds(pl.multiple_of(grp * LANES, LANES), LANES)] = acc.astype(BF16)
        acc_ref[...] = acc

    def field(item, f):
        return tab_ref[item * _ITEM_FIELDS + f]

    def iteration(i, parity):
        p = parity
        stage_logs(field(i + 2, 2), p)
        if parity == 0:
            stage_weights(True, p)
        else:
            stage_weights(False, p)
            carry_all[field(i + 1, 3)] = carry_ref[...]
        stage_scores(field(i + 3, 0), field(i + 3, 1), field(i + 3, 4), p)
        stage_output(field(i, 0), field(i, 1), field(i, 4), parity == 1, p)
        if parity == 0:
            acc_all[field(i, 3)] = acc_ref[...]

    def body(k, c):
        for j in range(_UNROLL):
            iteration(_UNROLL * k + j, j % 2)
        return c

    lax.fori_loop(0, n_iter // _UNROLL, body, 0)

    def saturated():
        return jnp.max(carry_ref[...]) <= LOG_UNDERFLOW

    n_tiles = s // t
    first_open = _PIPELINED_BLOCKS - 1

    def finish_tile(k, c):
        slot = open_ref[k]
        grp = slot // n_tiles
        qi = slot % n_tiles
        n_blocks = qi + 2
        carry_ref[...] = carry_all[slot]
        acc_ref[...] = acc_all[slot]
        q_off = qi * t

        def more(state):
            n, done = state
            return jnp.logical_and(n < n_blocks, jnp.logical_not(done))

        def one_block(state):
            n, _ = state
            is_meta = n == n_blocks - 1
            k_off = jnp.where(is_meta, s, (qi - n) * t)
            kind = jnp.where(is_meta, _KIND_META, _KIND_PLAIN)
            stage_scores(q_off, k_off, grp, 0)
            stage_logs(kind, 1)
            stage_weights(None, 0)
            stage_output(q_off, k_off, grp, None, 1)
            return n + 1, saturated()

        lax.while_loop(more, one_block, (jnp.int32(_PIPELINED_BLOCKS), jnp.bool_(False)))
        return c

    open_slots = [grp * n_tiles + qi for grp in range(STEP_GROUPS)
                  for qi in range(first_open, n_tiles)]
    worst = carry_all[open_slots[0]]
    for slot in open_slots[1:]:
        worst = jnp.maximum(worst, carry_all[slot])

    @pl.when(jnp.max(worst) > LOG_UNDERFLOW)
    def _():
        n_open = jnp.int32(0)
        for slot in open_slots:
            open_ref[n_open] = slot
            stays = jnp.max(carry_all[slot]) > LOG_UNDERFLOW
            n_open = n_open + stays.astype(jnp.int32)
        lax.fori_loop(0, n_open, finish_tile, 0)


def _attention(qkv, kt, meta_qkv, meta_kt, *, t):
    b, s, _ = qkv.shape
    d = D_MODEL
    nh = HEADS_PER_GROUP
    table, n_iter = _attention_items(s, t)
    u = _suffix_sum_matrix(t, -1.0)
    bias = _mask_bias(t)
    width = STEP_GROUPS * LANES
    col0 = d // width
    n_slots = STEP_GROUPS * (s // t)
    grid_spec = pltpu.PrefetchScalarGridSpec(
        num_scalar_prefetch=1,
        grid=(b, N_GROUPS // STEP_GROUPS),
        in_specs=[
            pl.BlockSpec((None, s, width), lambda bi, g, tab: (bi, 0, g)),
            pl.BlockSpec((None, width, s), lambda bi, g, tab: (bi, g, 0)),
            pl.BlockSpec((None, s, width), lambda bi, g, tab: (bi, 0, 2 * col0 + g)),
            pl.BlockSpec((width, META_ROWS), lambda bi, g, tab: (g, 0)),
            pl.BlockSpec((META_ROWS, width), lambda bi, g, tab: (0, 2 * col0 + g)),
            pl.BlockSpec(u.shape, lambda bi, g, tab: (0, 0)),
            pl.BlockSpec(bias.shape, lambda bi, g, tab: (0, 0, 0)),
        ],
        out_specs=pl.BlockSpec((None, s, width), lambda bi, g, tab: (bi, 0, g)),
        scratch_shapes=[
            pltpu.VMEM((STEP_GROUPS * nh, LANES, s + t), BF16),
            pltpu.VMEM((STEP_GROUPS * nh, s + t, LANES), BF16),
            pltpu.VMEM((2, nh, t, t), F32),
            pltpu.VMEM((2, nh, t, t), F32),
            pltpu.VMEM((2, nh, t, t), BF16),
            pltpu.VMEM((2, nh, t, t), BF16),
            pltpu.VMEM((nh, t, LANES), F32),
            pltpu.VMEM((t, LANES), F32),
            pltpu.VMEM((n_slots, nh, t, LANES), F32),
            pltpu.VMEM((n_slots, t, LANES), F32),
            pltpu.SMEM((n_slots,), jnp.int32),
        ],
    )
    kernel = functools.partial(_attn_pipe_kernel, s=s, t=t, n_iter=n_iter)
    return pl.pallas_call(
        kernel,
        grid_spec=grid_spec,
        out_shape=jax.ShapeDtypeStruct((b, s, d), BF16),
        compiler_params=_params("arbitrary", "arbitrary"),
        name="stick_breaking_attention",
    )(jnp.asarray(table), qkv, kt, qkv, meta_kt, meta_qkv, u, bias)


def kernel(x, meta_tokens, conv_norm, conv_w_in, conv_w, conv_w_out, attn_norm,
           attn_w_qkv, attn_w_out, mlp_norm, mlp_w1, mlp_w2, final_norm):
    b, s, d = x.shape
    depth = mlp_norm.shape[0]
    hr = x.reshape(b * s, d)
    hm = jnp.pad(meta_tokens.astype(x.dtype), ((0, META_ROWS - N_META), (0, 0)))
    zero_hist = jnp.zeros((SUBLANES, d), F32)
    gain = lambda g: g.reshape(1, d)

    stacks = {"w_in": conv_w_in, "w_out": conv_w_out, "w_qkv": attn_w_qkv,
              "w_o": attn_w_out, "w1": mlp_w1, "w2": mlp_w2}
    is_conv = lambda i: i % 2 == 0

    def mixer_weights(i):
        return [("w_in", i // 2), ("w_out", i // 2)] if is_conv(i) else [("w_qkv", i // 2)]

    def mlp_weights(i):
        return ([] if is_conv(i) else [("w_o", i // 2)]) + [("w1", i), ("w2", i)]

    first = mixer_weights(0) + ([] if is_conv(0) else mlp_weights(0))
    bf = {key: stacks[key[0]][key[1]].astype(BF16) for key in first}

    def with_casts(call, keys):
        out, *cast = call(casts=[(stacks[name], layer) for name, layer in keys])
        bf.update(zip(keys, cast))
        return out

    for i in range(depth):
        j = i // 2
        last = i == depth - 1
        final_g = gain(final_norm) if last else None
        g_mlp = gain(mlp_norm[i])
        nxt = [] if last else mixer_weights(i + 1) + ([] if is_conv(i + 1) else mlp_weights(i + 1))
        if is_conv(i):
            g = gain(conv_norm[j])
            w_in, w_out = bf["w_in", j], bf["w_out", j]
            hm, hist = _conv_mixer(hm, g, w_in, conv_w[j], w_out, zero_hist,
                                   tm=META_ROWS, tiles_per_seq=1, hist_row=N_META)
            hr = with_casts(functools.partial(
                _conv_mixer, hr, g, w_in, conv_w[j], w_out, hist,
                tm=WIDE_ROW_TILE, tiles_per_seq=s // WIDE_ROW_TILE), mlp_weights(i))
            (hm,) = _mlp(hm, g_mlp, bf["w1", i], bf["w2", i], tm=META_ROWS)
            hr = with_casts(functools.partial(
                _mlp, hr, g_mlp, bf["w1", i], bf["w2", i], tm=ROW_TILE, final_g=final_g), nxt)
        else:
            g = gain(attn_norm[j])
            qkv_m, kt_m = _qkv(hm, g, bf["w_qkv", j], tm=META_ROWS, seq=META_ROWS)
            qkv_r, kt_r = _qkv(hr, g, bf["w_qkv", j], tm=WIDE_ROW_TILE, seq=s)
            o_r = _attention(qkv_r.reshape(b, s, 3 * d), kt_r, qkv_m, kt_m[0], t=ATTN_TILE)
            if not last:
                o_m = _attention_small(qkv_m)
                (hm,) = _mlp(hm, g_mlp, bf["w1", i], bf["w2", i], tm=META_ROWS,
                             attn=o_m, w_o=bf["w_o", j])
            hr = with_casts(functools.partial(
                _mlp, hr, g_mlp, bf["w1", i], bf["w2", i], tm=ROW_TILE,
                attn=o_r.reshape(b * s, d), w_o=bf["w_o", j], final_g=final_g), nxt)
    return hr.reshape(b, s, d)
```

```python
import functools
import math

import numpy as np
import jax
import jax.numpy as jnp
from jax import lax
from jax.experimental import pallas as pl
from jax.experimental.pallas import tpu as pltpu

D_MODEL = 1024
N_META = 16
N_HEADS = 16
HEAD_DIM = D_MODEL // N_HEADS
CONV_W = 3
D_FF = 4 * D_MODEL
RMS_EPS = 1e-6

LANES = 128
SUBLANES = 8
HEADS_PER_GROUP = LANES // HEAD_DIM
N_GROUPS = N_HEADS // HEADS_PER_GROUP
META_ROWS = LANES
ROW_TILE = 1024
WIDE_ROW_TILE = 1024
ATTN_TILE = 256
FF_CHUNK = 512
VMEM_LIMIT_BYTES = 56 * 1024 * 1024
LOG2_E = math.log2(math.e)
MASK_BIAS = -1e30

BF16 = jnp.bfloat16
F32 = jnp.float32


def _dot(a, b):
    return jnp.dot(a, b, preferred_element_type=F32)


def _rmsnorm(x, g):
    return x * lax.rsqrt(jnp.mean(x * x, axis=-1, keepdims=True) + RMS_EPS) * g


def _resident(shape):
    zeros = (0,) * len(shape)
    return pl.BlockSpec(shape, lambda *_: zeros, pipeline_mode=pl.Buffered(1))


def _cast_specs(casts, n_steps):
    in_specs, out_specs, out_shapes = [], [], []
    for stack, layer in casts:
        _, rows, cols = stack.shape
        chunk = rows // n_steps
        assert chunk * n_steps == rows and chunk % (2 * SUBLANES) == 0, (rows, n_steps)
        in_specs.append(pl.BlockSpec((None, chunk, cols), lambda i, layer=layer: (layer, i, 0)))
        out_specs.append(pl.BlockSpec((chunk, cols), lambda i: (i, 0)))
        out_shapes.append(jax.ShapeDtypeStruct((rows, cols), BF16))
    return in_specs, out_specs, out_shapes


def _do_casts(src_refs, dst_refs):
    for src, dst in zip(src_refs, dst_refs, strict=True):
        dst[...] = src[...].astype(BF16)


def _params(*semantics):
    return pltpu.CompilerParams(dimension_semantics=semantics,
                                vmem_limit_bytes=VMEM_LIMIT_BYTES)


def _conv_mixer_kernel(x_ref, g_ref, w_in_ref, cw_ref, w_out_ref, hist_ref,
                       *rest, tm, tiles_per_seq, hist_row, n_casts):
    rest = list(rest)
    cast_src = [rest.pop(0) for _ in range(n_casts)]
    o_ref = rest.pop(0)
    hist_out_ref = None if hist_row is None else rest.pop(0)
    cast_dst = [rest.pop(0) for _ in range(n_casts)]
    (ubuf,) = rest
    _do_casts(cast_src, cast_dst)
    i = pl.program_id(0)
    first = (i % tiles_per_seq) == 0

    @pl.when(first)
    def _():
        ubuf[0:SUBLANES, :] = hist_ref[...]

    @pl.when(jnp.logical_not(first))
    def _():
        ubuf[0:SUBLANES, :] = ubuf[tm:tm + SUBLANES, :]

    x = x_ref[...]
    h = _rmsnorm(x, g_ref[...]).astype(BF16)
    d = D_MODEL
    gate_c = _dot(h, w_in_ref[:, d:2 * d])
    val = _dot(h, w_in_ref[:, 2 * d:3 * d])
    ubuf[SUBLANES:SUBLANES + tm, :] = gate_c * val
    if hist_row is not None:
        hist_out_ref[...] = ubuf[hist_row:hist_row + SUBLANES, :]
    cw = cw_ref[...]
    conv = ubuf[SUBLANES:SUBLANES + tm, :] * cw[CONV_W - 1:CONV_W, :]
    for k in range(CONV_W - 1):
        back = CONV_W - 1 - k
        conv = conv + ubuf[SUBLANES - back:SUBLANES - back + tm, :] * cw[k:k + 1, :]
    gate_b = _dot(h, w_in_ref[:, 0:d])
    y = (gate_b * conv).astype(BF16)
    o_ref[...] = x + _dot(y, w_out_ref[...])


def _conv_mixer(x, g, w_in, cw, w_out, hist, *, tm, tiles_per_seq, hist_row=None, casts=()):
    n, d = x.shape
    n_steps = n // tm
    kernel = functools.partial(_conv_mixer_kernel, tm=tm, tiles_per_seq=tiles_per_seq,
                               hist_row=hist_row, n_casts=len(casts))
    row = pl.BlockSpec((tm, d), lambda i: (i, 0))
    cast_in, cast_out, cast_shapes = _cast_specs(casts, n_steps)
    out_shapes = [jax.ShapeDtypeStruct((n, d), F32)]
    out_specs = [row]
    if hist_row is not None:
        out_shapes.append(jax.ShapeDtypeStruct((SUBLANES, d), F32))
        out_specs.append(pl.BlockSpec((SUBLANES, d), lambda i: (0, 0)))
    return pl.pallas_call(
        kernel,
        grid=(n_steps,),
        in_specs=[row, _resident((1, d)), _resident((d, 3 * d)), _resident((CONV_W, d)),
                  _resident((d, d)), _resident((SUBLANES, d))] + cast_in,
        out_specs=out_specs + cast_out,
        out_shape=out_shapes + cast_shapes,
        scratch_shapes=[pltpu.VMEM((tm + 2 * SUBLANES, d), F32)],
        compiler_params=_params("arbitrary"),
        name="conv_mixer",
    )(x, g, w_in, cw, w_out, hist, *[stack for stack, _ in casts])


def _mlp_kernel(*refs, with_proj, with_final_norm, n_casts):
    refs = list(refs)
    x_ref = refs.pop(0)
    if with_proj:
        a_ref, w_o_ref = refs.pop(0), refs.pop(0)
    g_ref, w1_ref, w2_ref = refs.pop(0), refs.pop(0), refs.pop(0)
    if with_final_norm:
        gf_ref = refs.pop(0)
    cast_src = [refs.pop(0) for _ in range(n_casts)]
    o_ref = refs.pop(0)
    _do_casts(cast_src, refs)

    x = x_ref[...]
    if with_proj:
        x = x + _dot(a_ref[...], w_o_ref[...])
    h = _rmsnorm(x, g_ref[...]).astype(BF16)
    acc = x
    for c in range(D_FF // FF_CHUNK):
        cols = slice(c * FF_CHUNK, (c + 1) * FF_CHUNK)
        a = jnp.maximum(_dot(h, w1_ref[:, cols]), 0.0)
        acc = acc + _dot((a * a).astype(BF16), w2_ref[cols, :])
    if with_final_norm:
        acc = _rmsnorm(acc, gf_ref[...])
    o_ref[...] = acc


def _mlp(x, g, w1, w2, *, tm, attn=None, w_o=None, final_g=None, casts=()):
    n, d = x.shape
    n_steps = n // tm
    with_proj = attn is not None
    with_final_norm = final_g is not None
    row = pl.BlockSpec((tm, d), lambda i: (i, 0))
    args, specs = [x], [row]
    if with_proj:
        args += [attn, w_o]
        specs += [row, _resident((d, d))]
    args += [g, w1, w2]
    specs += [_resident((1, d)), _resident((d, D_FF)), _resident((D_FF, d))]
    if with_final_norm:
        args.append(final_g)
        specs.append(_resident((1, d)))
    cast_in, cast_out, cast_shapes = _cast_specs(casts, n_steps)
    kernel = functools.partial(_mlp_kernel, with_proj=with_proj,
                               with_final_norm=with_final_norm, n_casts=len(casts))
    return pl.pallas_call(
        kernel,
        grid=(n_steps,),
        in_specs=specs + cast_in,
        out_specs=[row] + cast_out,
        out_shape=[jax.ShapeDtypeStruct((n, d), F32)] + cast_shapes,
        compiler_params=_params("arbitrary"),
        name="mlp",
    )(*args, *[stack for stack, _ in casts])


def _qkv_kernel(x_ref, g_ref, w_qkv_ref, qkv_ref, kt_ref):
    d = D_MODEL
    h = _rmsnorm(x_ref[...], g_ref[...]).astype(BF16)
    scale = 1.0 / math.sqrt(HEAD_DIM)
    qkv_ref[:, 0:d] = (_dot(h, w_qkv_ref[:, 0:d]) * scale).astype(BF16)
    qkv_ref[:, d:3 * d] = _dot(h, w_qkv_ref[:, d:3 * d]).astype(BF16)
    for c in range(0, d, LANES):
        kt_ref[c:c + LANES, :] = qkv_ref[:, d + c:d + c + LANES].T


def _qkv(x, g, w_qkv, *, tm, seq):
    n, d = x.shape
    tiles_per_seq = seq // tm
    return pl.pallas_call(
        _qkv_kernel,
        grid=(n // tm,),
        in_specs=[pl.BlockSpec((tm, d), lambda i: (i, 0)), _resident((1, d)),
                  _resident((d, 3 * d))],
        out_specs=[pl.BlockSpec((tm, 3 * d), lambda i: (i, 0)),
                   pl.BlockSpec((None, d, tm),
                                lambda i: (i // tiles_per_seq, 0, i % tiles_per_seq))],
        out_shape=[jax.ShapeDtypeStruct((n, 3 * d), BF16),
                   jax.ShapeDtypeStruct((n // seq, d, seq), BF16)],
        compiler_params=_params("arbitrary"),
        name="qkv_proj",
    )(x, g, w_qkv)


def _suffix_sum_matrix(t, sign):
    r = jnp.arange(t)[:, None]
    c = jnp.arange(t + LANES)[None, :]
    return jnp.where((c >= t) | (r > c), sign, 0.0).astype(BF16)


def _head_of_lane(shape):
    return lax.broadcasted_iota(jnp.int32, shape, len(shape) - 1) // HEAD_DIM


def _attn_small_kernel(q_ref, k_ref, v_ref, u_ref, o_ref, *, t):
    q = q_ref[...]
    k = k_ref[...]
    head_of_lane = _head_of_lane((t, LANES))
    row_i = lax.broadcasted_iota(jnp.int32, (t, t), 0)
    col_i = lax.broadcasted_iota(jnp.int32, (t, t), 1)
    causal = col_i < row_i
    out = None
    for hh in range(HEADS_PER_GROUP):
        qh = jnp.where(head_of_lane == hh, q, jnp.zeros_like(q))
        z = lax.dot_general(qh, k, (((1,), (1,)), ((), ())), preferred_element_type=F32)
        soft = jnp.log(1.0 + jnp.exp(-jnp.abs(z)))
        log_beta = jnp.minimum(z, 0.0) - soft
        log_1m = jnp.where(causal, log_beta - z, 0.0)
        tail = _dot(log_1m.astype(BF16), u_ref[:, 0:t])
        w = jnp.where(causal, jnp.exp(log_beta + tail), 0.0)
        oh = _dot(w.astype(BF16), v_ref[...])
        out = oh if out is None else jnp.where(head_of_lane == hh, oh, out)
    o_ref[...] = out.astype(BF16)


def _attention_small(qkv):
    t = qkv.shape[0]
    d = D_MODEL
    u = _suffix_sum_matrix(t, 1.0)
    col0 = d // LANES
    return pl.pallas_call(
        functools.partial(_attn_small_kernel, t=t),
        grid=(N_GROUPS,),
        in_specs=[pl.BlockSpec((t, LANES), lambda g: (0, g)),
                  pl.BlockSpec((t, LANES), lambda g: (0, col0 + g)),
                  pl.BlockSpec((t, LANES), lambda g: (0, 2 * col0 + g)),
                  _resident(u.shape)],
        out_specs=pl.BlockSpec((t, LANES), lambda g: (0, g)),
        out_shape=jax.ShapeDtypeStruct((t, d), BF16),
        compiler_params=_params("arbitrary"),
        name="attention_meta",
    )(qkv, qkv, qkv, u)


_KIND_PLAIN, _KIND_DIAG, _KIND_META = 0, 1, 2
_PIPE_DEPTH = 4
_UNROLL = 34
_ITEM_FIELDS = 5
_PIPELINED_BLOCKS = 2
assert _PIPELINED_BLOCKS == 2 and _PIPE_DEPTH % 2 == 0 and _UNROLL % 2 == 0
STEP_GROUPS = 4
LOG_UNDERFLOW = -105.0


def _tile_blocks(qi, s, t):
    blocks = [(qi * t, _KIND_DIAG)]
    blocks += [(j * t, _KIND_PLAIN) for j in range(qi - 1, -1, -1)]
    blocks.append((s, _KIND_META))
    return blocks


def _attention_items(s, t):
    n_tiles = s // t
    items = []
    for grp in range(STEP_GROUPS):
        for qi in range(n_tiles):
            for k_off, kind in _tile_blocks(qi, s, t)[:_PIPELINED_BLOCKS]:
                items.append([qi * t, k_off, kind, grp * n_tiles + qi, grp])
    lead = _PIPE_DEPTH - 1
    n_iter = len(items) + lead
    n_iter += -n_iter % _UNROLL
    first_pair, last_pair = items[:2], items[-2:]
    table = (first_pair * 2)[-lead:] + items
    while len(table) < n_iter + lead:
        table.append(last_pair[(len(table) - lead) % 2])
    return np.asarray(table, np.int32).reshape(-1), n_iter


def _mask_bias(t):
    r = np.arange(t)[:, None]
    c = np.arange(t)[None, :]
    bias = np.zeros((3, t, t), np.float32)
    bias[_KIND_DIAG] = np.where(c < r, 0.0, MASK_BIAS)
    bias[_KIND_META] = np.where(c < N_META, 0.0, MASK_BIAS) + 0.0 * r
    return jnp.asarray(bias)


def _attn_pipe_kernel(tab_ref, q_ref, kt_ref, v_ref, ktm_ref, vm_ref, u_ref, bias_ref,
                      o_ref, kt_all, v_all, zbuf, lbuf, pbuf, wbuf, carry_ref,
                      acc_ref, carry_all, acc_all, open_ref, *, s, t, n_iter):
    nh = HEADS_PER_GROUP
    first_step = jnp.logical_and(pl.program_id(0) == 0, pl.program_id(1) == 0)

    @pl.when(first_step)
    def _():
        for ref in (zbuf, lbuf, pbuf, wbuf, carry_ref, acc_ref):
            ref[...] = jnp.zeros_like(ref)
        kt_all[...] = jnp.zeros_like(kt_all)
        v_all[:, s + META_ROWS:s + t, :] = jnp.zeros(
            (STEP_GROUPS * nh, t - META_ROWS, LANES), BF16)

    head_of_lane = _head_of_lane((1, LANES))
    for grp in range(STEP_GROUPS):
        cols = slice(grp * LANES, (grp + 1) * LANES)
        v = v_ref[:, cols]
        vm = vm_ref[:, cols]
        for hh in range(nh):
            mine = head_of_lane == hh
            v_all[grp * nh + hh, 0:s, :] = jnp.where(mine, v, jnp.zeros_like(v))
            v_all[grp * nh + hh, s:s + META_ROWS, :] = jnp.where(mine, vm, jnp.zeros_like(vm))
        for hh in range(nh):
            rows = slice(hh * HEAD_DIM, (hh + 1) * HEAD_DIM)
            src = slice(grp * LANES + hh * HEAD_DIM, grp * LANES + (hh + 1) * HEAD_DIM)
            kt_all[grp * nh + hh, rows, 0:s] = kt_ref[src, :]
            kt_all[grp * nh + hh, rows, s:s + META_ROWS] = ktm_ref[src, :]

    def stage_scores(q_off, k_off, grp, p):
        q_blk = q_ref[pl.ds(pl.multiple_of(q_off, t), t),
                      pl.ds(pl.multiple_of(grp * LANES, LANES), LANES)]
        for hh in range(nh):
            zbuf[p, hh] = _dot(q_blk, kt_all[grp * nh + hh, :, pl.ds(pl.multiple_of(k_off, t), t)])

    def stage_logs(kind, p):
        bias = bias_ref[kind]
        for hh in range(nh):
            z = zbuf[1 - p, hh] + bias
            soft = jnp.log(1.0 + jnp.exp2(jnp.abs(z) * (-LOG2_E)))
            neg_log_1m = jnp.maximum(z, 0.0) + soft
            lbuf[p, hh] = z - neg_log_1m
            pbuf[p, hh] = neg_log_1m.astype(BF16)

    def stage_weights(first, p):
        for hh in range(nh):
            sums = _dot(pbuf[1 - p, hh], u_ref[...])
            tail, row_sum = sums[:, 0:t], sums[:, t:t + LANES]
            if first:
                carry_ref[hh] = row_sum
            else:
                carry = carry_ref[hh]
                tail = tail + jnp.concatenate([carry] * (t // LANES), axis=1)
                carry_ref[hh] = carry + row_sum
            wbuf[p, hh] = jnp.exp(lbuf[1 - p, hh] + tail).astype(BF16)

    def stage_output(q_off, k_off, grp, first, p):
        k_off = pl.multiple_of(k_off, t)
        w_cat = jnp.concatenate([wbuf[1 - p, hh] for hh in range(nh)], axis=1)
        v_cat = jnp.concatenate([v_all[grp * nh + hh, pl.ds(k_off, t), :]
                                 for hh in range(nh)], axis=0)
        acc = _dot(w_cat, v_cat)
        if not first:
            acc = acc_ref[...] + acc
            o_ref[pl.ds(pl.multiple_of(q_off, t), t),
                  pl.ds(pl.multiple_of(grp * LANES, LANES), LANES)] = acc.astype(BF16)
        acc_ref[...] = acc

    def field(item, f):
        return tab_ref[item * _ITEM_FIELDS + f]

    def iteration(i, parity):
        p = parity
        stage_logs(field(i + 2, 2), p)
        if parity == 0:
            stage_weights(True, p)
        else:
            stage_weights(False, p)
            carry_all[field(i + 1, 3)] = carry_ref[...]
        stage_scores(field(i + 3, 0), field(i + 3, 1), field(i + 3, 4), p)
        stage_output(field(i, 0), field(i, 1), field(i, 4), parity == 1, p)
        if parity == 0:
            acc_all[field(i, 3)] = acc_ref[...]

    def body(k, c):
        for j in range(_UNROLL):
            iteration(_UNROLL * k + j, j % 2)
        return c

    lax.fori_loop(0, n_iter // _UNROLL, body, 0)

    def saturated():
        return jnp.max(carry_ref[...]) <= LOG_UNDERFLOW

    n_tiles = s // t
    first_open = _PIPELINED_BLOCKS - 1

    def finish_tile(k, c):
        slot = open_ref[k]
        grp = slot // n_tiles
        qi = slot % n_tiles
        n_blocks = qi + 2
        carry_ref[...] = carry_all[slot]
        acc_ref[...] = acc_all[slot]
        q_off = qi * t

        def more(state):
            n, done = state
            return jnp.logical_and(n < n_blocks, jnp.logical_not(done))

        def one_block(state):
            n, _ = state
            is_meta = n == n_blocks - 1
            k_off = jnp.where(is_meta, s, (qi - n) * t)
            kind = jnp.where(is_meta, _KIND_META, _KIND_PLAIN)
            stage_scores(q_off, k_off, grp, 0)
            stage_logs(kind, 1)
            stage_weights(None, 0)
            stage_output(q_off, k_off, grp, None, 1)
            return n + 1, saturated()

        lax.while_loop(more, one_block, (jnp.int32(_PIPELINED_BLOCKS), jnp.bool_(False)))
        return c

    open_slots = [grp * n_tiles + qi for grp in range(STEP_GROUPS)
                  for qi in range(first_open, n_tiles)]
    worst = carry_all[open_slots[0]]
    for slot in open_slots[1:]:
        worst = jnp.maximum(worst, carry_all[slot])

    @pl.when(jnp.max(worst) > LOG_UNDERFLOW)
    def _():
        n_open = jnp.int32(0)
        for slot in open_slots:
            open_ref[n_open] = slot
            stays = jnp.max(carry_all[slot]) > LOG_UNDERFLOW
            n_open = n_open + stays.astype(jnp.int32)
        lax.fori_loop(0, n_open, finish_tile, 0)


def _attention(qkv, kt, meta_qkv, meta_kt, *, t):
    b, s, _ = qkv.shape
    d = D_MODEL
    nh = HEADS_PER_GROUP
    table, n_iter = _attention_items(s, t)
    u = _suffix_sum_matrix(t, -1.0)
    bias = _mask_bias(t)
    width = STEP_GROUPS * LANES
    col0 = d // width
    n_slots = STEP_GROUPS * (s // t)
    grid_spec = pltpu.PrefetchScalarGridSpec(
        num_scalar_prefetch=1,
        grid=(b, N_GROUPS // STEP_GROUPS),
        in_specs=[
            pl.BlockSpec((None, s, width), lambda bi, g, tab: (bi, 0, g)),
            pl.BlockSpec((None, width, s), lambda bi, g, tab: (bi, g, 0)),
            pl.BlockSpec((None, s, width), lambda bi, g, tab: (bi, 0, 2 * col0 + g)),
            pl.BlockSpec((width, META_ROWS), lambda bi, g, tab: (g, 0)),
            pl.BlockSpec((META_ROWS, width), lambda bi, g, tab: (0, 2 * col0 + g)),
            pl.BlockSpec(u.shape, lambda bi, g, tab: (0, 0)),
            pl.BlockSpec(bias.shape, lambda bi, g, tab: (0, 0, 0)),
        ],
        out_specs=pl.BlockSpec((None, s, width), lambda bi, g, tab: (bi, 0, g)),
        scratch_shapes=[
            pltpu.VMEM((STEP_GROUPS * nh, LANES, s + t), BF16),
            pltpu.VMEM((STEP_GROUPS * nh, s + t, LANES), BF16),
            pltpu.VMEM((2, nh, t, t), F32),
            pltpu.VMEM((2, nh, t, t), F32),
            pltpu.VMEM((2, nh, t, t), BF16),
            pltpu.VMEM((2, nh, t, t), BF16),
            pltpu.VMEM((nh, t, LANES), F32),
            pltpu.VMEM((t, LANES), F32),
            pltpu.VMEM((n_slots, nh, t, LANES), F32),
            pltpu.VMEM((n_slots, t, LANES), F32),
            pltpu.SMEM((n_slots,), jnp.int32),
        ],
    )
    kernel = functools.partial(_attn_pipe_kernel, s=s, t=t, n_iter=n_iter)
    return pl.pallas_call(
        kernel,
        grid_spec=grid_spec,
        out_shape=jax.ShapeDtypeStruct((b, s, d), BF16),
        compiler_params=_params("arbitrary", "arbitrary"),
        name="stick_breaking_attention",
    )(jnp.asarray(table), qkv, kt, qkv, meta_kt, meta_qkv, u, bias)


def kernel(x, meta_tokens, conv_norm, conv_w_in, conv_w, conv_w_out, attn_norm,
           attn_w_qkv, attn_w_out, mlp_norm, mlp_w1, mlp_w2, final_norm):
    b, s, d = x.shape
    depth = mlp_norm.shape[0]
    hr = x.reshape(b * s, d)
    hm = jnp.pad(meta_tokens.astype(x.dtype), ((0, META_ROWS - N_META), (0, 0)))
    zero_hist = jnp.zeros((SUBLANES, d), F32)
    gain = lambda g: g.reshape(1, d)

    stacks = {"w_in": conv_w_in, "w_out": conv_w_out, "w_qkv": attn_w_qkv,
              "w_o": attn_w_out, "w1": mlp_w1, "w2": mlp_w2}
    is_conv = lambda i: i % 2 == 0

    def mixer_weights(i):
        return [("w_in", i // 2), ("w_out", i // 2)] if is_conv(i) else [("w_qkv", i // 2)]

    def mlp_weights(i):
        return ([] if is_conv(i) else [("w_o", i // 2)]) + [("w1", i), ("w2", i)]

    first = mixer_weights(0) + ([] if is_conv(0) else mlp_weights(0))
    bf = {key: stacks[key[0]][key[1]].astype(BF16) for key in first}

    def with_casts(call, keys):
        out, *cast = call(casts=[(stacks[name], layer) for name, layer in keys])
        bf.update(zip(keys, cast))
        return out

    for i in range(depth):
        j = i // 2
        last = i == depth - 1
        final_g = gain(final_norm) if last else None
        g_mlp = gain(mlp_norm[i])
        nxt = [] if last else mixer_weights(i + 1) + ([] if is_conv(i + 1) else mlp_weights(i + 1))
        if is_conv(i):
            g = gain(conv_norm[j])
            w_in, w_out = bf["w_in", j], bf["w_out", j]
            hm, hist = _conv_mixer(hm, g, w_in, conv_w[j], w_out, zero_hist,
                                   tm=META_ROWS, tiles_per_seq=1, hist_row=N_META)
            hr = with_casts(functools.partial(
                _conv_mixer, hr, g, w_in, conv_w[j], w_out, hist,
                tm=WIDE_ROW_TILE, tiles_per_seq=s // WIDE_ROW_TILE), mlp_weights(i))
            (hm,) = _mlp(hm, g_mlp, bf["w1", i], bf["w2", i], tm=META_ROWS)
            hr = with_casts(functools.partial(
                _mlp, hr, g_mlp, bf["w1", i], bf["w2", i], tm=ROW_TILE, final_g=final_g), nxt)
        else:
            g = gain(attn_norm[j])
            qkv_m, kt_m = _qkv(hm, g, bf["w_qkv", j], tm=META_ROWS, seq=META_ROWS)
            qkv_r, kt_r = _qkv(hr, g, bf["w_qkv", j], tm=WIDE_ROW_TILE, seq=s)
            o_r = _attention(qkv_r.reshape(b, s, 3 * d), kt_r, qkv_m, kt_m[0], t=ATTN_TILE)
            if not last:
                o_m = _attention_small(qkv_m)
                (hm,) = _mlp(hm, g_mlp, bf["w1", i], bf["w2", i], tm=META_ROWS,
                             attn=o_m, w_o=bf["w_o", j])
            hr = with_casts(functools.partial(
                _mlp, hr, g_mlp, bf["w1", i], bf["w2", i], tm=ROW_TILE,
                attn=o_r.reshape(b * s, d), w_o=bf["w_o", j], final_g=final_g), nxt)
    return hr.reshape(b, s, d)
```

```python
import functools
import math

import numpy as np
import jax
import jax.numpy as jnp
from jax import lax
from jax.experimental import pallas as pl
from jax.experimental.pallas import tpu as pltpu

D_MODEL = 1024
N_META = 16
N_HEADS = 16
HEAD_DIM = D_MODEL // N_HEADS
CONV_W = 3
D_FF = 4 * D_MODEL
RMS_EPS = 1e-6

LANES = 128
SUBLANES = 8
HEADS_PER_GROUP = LANES // HEAD_DIM
N_GROUPS = N_HEADS // HEADS_PER_GROUP
META_ROWS = LANES
ROW_TILE = 1024
WIDE_ROW_TILE = 1024
ATTN_TILE = 256
FF_CHUNK = 512
VMEM_LIMIT_BYTES = 56 * 1024 * 1024
LOG2_E = math.log2(math.e)
MASK_BIAS = -1e30

BF16 = jnp.bfloat16
F32 = jnp.float32


def _dot(a, b):
    return jnp.dot(a, b, preferred_element_type=F32)


def _rmsnorm(x, g):
    return x * lax.rsqrt(jnp.mean(x * x, axis=-1, keepdims=True) + RMS_EPS) * g


def _resident(shape):
    zeros = (0,) * len(shape)
    return pl.BlockSpec(shape, lambda *_: zeros, pipeline_mode=pl.Buffered(1))


def _cast_specs(casts, n_steps):
    in_specs, out_specs, out_shapes = [], [], []
    for stack, layer in casts:
        _, rows, cols = stack.shape
        chunk = rows // n_steps
        assert chunk * n_steps == rows and chunk % (2 * SUBLANES) == 0, (rows, n_steps)
        in_specs.append(pl.BlockSpec((None, chunk, cols), lambda i, layer=layer: (layer, i, 0)))
        out_specs.append(pl.BlockSpec((chunk, cols), lambda i: (i, 0)))
        out_shapes.append(jax.ShapeDtypeStruct((rows, cols), BF16))
    return in_specs, out_specs, out_shapes


def _do_casts(src_refs, dst_refs):
    for src, dst in zip(src_refs, dst_refs, strict=True):
        dst[...] = src[...].astype(BF16)


def _params(*semantics):
    return pltpu.CompilerParams(dimension_semantics=semantics,
                                vmem_limit_bytes=VMEM_LIMIT_BYTES)


def _conv_mixer_kernel(x_ref, g_ref, w_in_ref, cw_ref, w_out_ref, hist_ref,
                       *rest, tm, tiles_per_seq, hist_row, n_casts):
    rest = list(rest)
    cast_src = [rest.pop(0) for _ in range(n_casts)]
    o_ref = rest.pop(0)
    hist_out_ref = None if hist_row is None else rest.pop(0)
    cast_dst = [rest.pop(0) for _ in range(n_casts)]
    (ubuf,) = rest
    _do_casts(cast_src, cast_dst)
    i = pl.program_id(0)
    first = (i % tiles_per_seq) == 0

    @pl.when(first)
    def _():
        ubuf[0:SUBLANES, :] = hist_ref[...]

    @pl.when(jnp.logical_not(first))
    def _():
        ubuf[0:SUBLANES, :] = ubuf[tm:tm + SUBLANES, :]

    x = x_ref[...]
    h = _rmsnorm(x, g_ref[...]).astype(BF16)
    d = D_MODEL
    gate_c = _dot(h, w_in_ref[:, d:2 * d])
    val = _dot(h, w_in_ref[:, 2 * d:3 * d])
    ubuf[SUBLANES:SUBLANES + tm, :] = gate_c * val
    if hist_row is not None:
        hist_out_ref[...] = ubuf[hist_row:hist_row + SUBLANES, :]
    cw = cw_ref[...]
    conv = ubuf[SUBLANES:SUBLANES + tm, :] * cw[CONV_W - 1:CONV_W, :]
    for k in range(CONV_W - 1):
        back = CONV_W - 1 - k
        conv = conv + ubuf[SUBLANES - back:SUBLANES - back + tm, :] * cw[k:k + 1, :]
    gate_b = _dot(h, w_in_ref[:, 0:d])
    y = (gate_b * conv).astype(BF16)
    o_ref[...] = x + _dot(y, w_out_ref[...])


def _conv_mixer(x, g, w_in, cw, w_out, hist, *, tm, tiles_per_seq, hist_row=None, casts=()):
    n, d = x.shape
    n_steps = n // tm
    kernel = functools.partial(_conv_mixer_kernel, tm=tm, tiles_per_seq=tiles_per_seq,
                               hist_row=hist_row, n_casts=len(casts))
    row = pl.BlockSpec((tm, d), lambda i: (i, 0))
    cast_in, cast_out, cast_shapes = _cast_specs(casts, n_steps)
    out_shapes = [jax.ShapeDtypeStruct((n, d), F32)]
    out_specs = [row]
    if hist_row is not None:
        out_shapes.append(jax.ShapeDtypeStruct((SUBLANES, d), F32))
        out_specs.append(pl.BlockSpec((SUBLANES, d), lambda i: (0, 0)))
    return pl.pallas_call(
        kernel,
        grid=(n_steps,),
        in_specs=[row, _resident((1, d)), _resident((d, 3 * d)), _resident((CONV_W, d)),
                  _resident((d, d)), _resident((SUBLANES, d))] + cast_in,
        out_specs=out_specs + cast_out,
        out_shape=out_shapes + cast_shapes,
        scratch_shapes=[pltpu.VMEM((tm + 2 * SUBLANES, d), F32)],
        compiler_params=_params("arbitrary"),
        name="conv_mixer",
    )(x, g, w_in, cw, w_out, hist, *[stack for stack, _ in casts])


def _mlp_kernel(*refs, with_proj, with_final_norm, n_casts):
    refs = list(refs)
    x_ref = refs.pop(0)
    if with_proj:
        a_ref, w_o_ref = refs.pop(0), refs.pop(0)
    g_ref, w1_ref, w2_ref = refs.pop(0), refs.pop(0), refs.pop(0)
    if with_final_norm:
        gf_ref = refs.pop(0)
    cast_src = [refs.pop(0) for _ in range(n_casts)]
    o_ref = refs.pop(0)
    _do_casts(cast_src, refs)

    x = x_ref[...]
    if with_proj:
        x = x + _dot(a_ref[...], w_o_ref[...])
    h = _rmsnorm(x, g_ref[...]).astype(BF16)
    acc = x
    for c in range(D_FF // FF_CHUNK):
        cols = slice(c * FF_CHUNK, (c + 1) * FF_CHUNK)
        a = jnp.maximum(_dot(h, w1_ref[:, cols]), 0.0)
        acc = acc + _dot((a * a).astype(BF16), w2_ref[cols, :])
    if with_final_norm:
        acc = _rmsnorm(acc, gf_ref[...])
    o_ref[...] = acc


def _mlp(x, g, w1, w2, *, tm, attn=None, w_o=None, final_g=None, casts=()):
    n, d = x.shape
    n_steps = n // tm
    with_proj = attn is not None
    with_final_norm = final_g is not None
    row = pl.BlockSpec((tm, d), lambda i: (i, 0))
    args, specs = [x], [row]
    if with_proj:
        args += [attn, w_o]
        specs += [row, _resident((d, d))]
    args += [g, w1, w2]
    specs += [_resident((1, d)), _resident((d, D_FF)), _resident((D_FF, d))]
    if with_final_norm:
        args.append(final_g)
        specs.append(_resident((1, d)))
    cast_in, cast_out, cast_shapes = _cast_specs(casts, n_steps)
    kernel = functools.partial(_mlp_kernel, with_proj=with_proj,
                               with_final_norm=with_final_norm, n_casts=len(casts))
    return pl.pallas_call(
        kernel,
        grid=(n_steps,),
        in_specs=specs + cast_in,
        out_specs=[row] + cast_out,
        out_shape=[jax.ShapeDtypeStruct((n, d), F32)] + cast_shapes,
        compiler_params=_params("arbitrary"),
        name="mlp",
    )(*args, *[stack for stack, _ in casts])


def _qkv_kernel(x_ref, g_ref, w_qkv_ref, qkv_ref, kt_ref):
    d = D_MODEL
    h = _rmsnorm(x_ref[...], g_ref[...]).astype(BF16)
    scale = 1.0 / math.sqrt(HEAD_DIM)
    qkv_ref[:, 0:d] = (_dot(h, w_qkv_ref[:, 0:d]) * scale).astype(BF16)
    qkv_ref[:, d:3 * d] = _dot(h, w_qkv_ref[:, d:3 * d]).astype(BF16)
    for c in range(0, d, LANES):
        kt_ref[c:c + LANES, :] = qkv_ref[:, d + c:d + c + LANES].T


def _qkv(x, g, w_qkv, *, tm, seq):
    n, d = x.shape
    tiles_per_seq = seq // tm
    return pl.pallas_call(
        _qkv_kernel,
        grid=(n // tm,),
        in_specs=[pl.BlockSpec((tm, d), lambda i: (i, 0)), _resident((1, d)),
                  _resident((d, 3 * d))],
        out_specs=[pl.BlockSpec((tm, 3 * d), lambda i: (i, 0)),
                   pl.BlockSpec((None, d, tm),
                                lambda i: (i // tiles_per_seq, 0, i % tiles_per_seq))],
        out_shape=[jax.ShapeDtypeStruct((n, 3 * d), BF16),
                   jax.ShapeDtypeStruct((n // seq, d, seq), BF16)],
        compiler_params=_params("arbitrary"),
        name="qkv_proj",
    )(x, g, w_qkv)


def _suffix_sum_matrix(t, sign):
    r = jnp.arange(t)[:, None]
    c = jnp.arange(t + LANES)[None, :]
    return jnp.where((c >= t) | (r > c), sign, 0.0).astype(BF16)


def _head_of_lane(shape):
    return lax.broadcasted_iota(jnp.int32, shape, len(shape) - 1) // HEAD_DIM


def _attn_small_kernel(q_ref, k_ref, v_ref, u_ref, o_ref, *, t):
    q = q_ref[...]
    k = k_ref[...]
    head_of_lane = _head_of_lane((t, LANES))
    row_i = lax.broadcasted_iota(jnp.int32, (t, t), 0)
    col_i = lax.broadcasted_iota(jnp.int32, (t, t), 1)
    causal = col_i < row_i
    out = None
    for hh in range(HEADS_PER_GROUP):
        qh = jnp.where(head_of_lane == hh, q, jnp.zeros_like(q))
        z = lax.dot_general(qh, k, (((1,), (1,)), ((), ())), preferred_element_type=F32)
        soft = jnp.log(1.0 + jnp.exp(-jnp.abs(z)))
        log_beta = jnp.minimum(z, 0.0) - soft
        log_1m = jnp.where(causal, log_beta - z, 0.0)
        tail = _dot(log_1m.astype(BF16), u_ref[:, 0:t])
        w = jnp.where(causal, jnp.exp(log_beta + tail), 0.0)
        oh = _dot(w.astype(BF16), v_ref[...])
        out = oh if out is None else jnp.where(head_of_lane == hh, oh, out)
    o_ref[...] = out.astype(BF16)


def _attention_small(qkv):
    t = qkv.shape[0]
    d = D_MODEL
    u = _suffix_sum_matrix(t, 1.0)
    col0 = d // LANES
    return pl.pallas_call(
        functools.partial(_attn_small_kernel, t=t),
        grid=(N_GROUPS,),
        in_specs=[pl.BlockSpec((t, LANES), lambda g: (0, g)),
                  pl.BlockSpec((t, LANES), lambda g: (0, col0 + g)),
                  pl.BlockSpec((t, LANES), lambda g: (0, 2 * col0 + g)),
                  _resident(u.shape)],
        out_specs=pl.BlockSpec((t, LANES), lambda g: (0, g)),
        out_shape=jax.ShapeDtypeStruct((t, d), BF16),
        compiler_params=_params("arbitrary"),
        name="attention_meta",
    )(qkv, qkv, qkv, u)


_KIND_PLAIN, _KIND_DIAG, _KIND_META = 0, 1, 2
_PIPE_DEPTH = 4
_UNROLL = 34
_ITEM_FIELDS = 5
_PIPELINED_BLOCKS = 2
assert _PIPELINED_BLOCKS == 2 and _PIPE_DEPTH % 2 == 0 and _UNROLL % 2 == 0
STEP_GROUPS = 4
LOG_UNDERFLOW = -105.0


def _tile_blocks(qi, s, t):
    blocks = [(qi * t, _KIND_DIAG)]
    blocks += [(j * t, _KIND_PLAIN) for j in range(qi - 1, -1, -1)]
    blocks.append((s, _KIND_META))
    return blocks


def _attention_items(s, t):
    n_tiles = s // t
    items = []
    for grp in range(STEP_GROUPS):
        for qi in range(n_tiles):
            for k_off, kind in _tile_blocks(qi, s, t)[:_PIPELINED_BLOCKS]:
                items.append([qi * t, k_off, kind, grp * n_tiles + qi, grp])
    lead = _PIPE_DEPTH - 1
    n_iter = len(items) + lead
    n_iter += -n_iter % _UNROLL
    first_pair, last_pair = items[:2], items[-2:]
    table = (first_pair * 2)[-lead:] + items
    while len(table) < n_iter + lead:
        table.append(last_pair[(len(table) - lead) % 2])
    return np.asarray(table, np.int32).reshape(-1), n_iter


def _mask_bias(t):
    r = np.arange(t)[:, None]
    c = np.arange(t)[None, :]
    bias = np.zeros((3, t, t), np.float32)
    bias[_KIND_DIAG] = np.where(c < r, 0.0, MASK_BIAS)
    bias[_KIND_META] = np.where(c < N_META, 0.0, MASK_BIAS) + 0.0 * r
    return jnp.asarray(bias)


def _attn_pipe_kernel(tab_ref, q_ref, kt_ref, v_ref, ktm_ref, vm_ref, u_ref, bias_ref,
                      o_ref, kt_all, v_all, zbuf, lbuf, pbuf, wbuf, carry_ref,
                      acc_ref, carry_all, acc_all, open_ref, rbuf, *, s, t, n_iter):
    nh = HEADS_PER_GROUP
    first_step = jnp.logical_and(pl.program_id(0) == 0, pl.program_id(1) == 0)

    @pl.when(first_step)
    def _():
        for ref in (zbuf, lbuf, pbuf, wbuf, rbuf, carry_ref, acc_ref):
            ref[...] = jnp.zeros_like(ref)
        kt_all[...] = jnp.zeros_like(kt_all)
        v_all[:, s + META_ROWS:s + t, :] = jnp.zeros(
            (STEP_GROUPS * nh, t - META_ROWS, LANES), BF16)

    head_of_lane = _head_of_lane((1, LANES))
    for grp in range(STEP_GROUPS):
        cols = slice(grp * LANES, (grp + 1) * LANES)
        v = v_ref[:, cols]
        vm = vm_ref[:, cols]
        for hh in range(nh):
            mine = head_of_lane == hh
            v_all[grp * nh + hh, 0:s, :] = jnp.where(mine, v, jnp.zeros_like(v))
            v_all[grp * nh + hh, s:s + META_ROWS, :] = jnp.where(mine, vm, jnp.zeros_like(vm))
        for hh in range(nh):
            rows = slice(hh * HEAD_DIM, (hh + 1) * HEAD_DIM)
            src = slice(grp * LANES + hh * HEAD_DIM, grp * LANES + (hh + 1) * HEAD_DIM)
            kt_all[grp * nh + hh, rows, 0:s] = kt_ref[src, :]
            kt_all[grp * nh + hh, rows, s:s + META_ROWS] = ktm_ref[src, :]

    def stage_scores(q_off, k_off, grp, p):
        q_blk = q_ref[pl.ds(pl.multiple_of(q_off, t), t),
                      pl.ds(pl.multiple_of(grp * LANES, LANES), LANES)]
        for hh in range(nh):
            zbuf[p, hh] = _dot(q_blk, kt_all[grp * nh + hh, :, pl.ds(pl.multiple_of(k_off, t), t)])

    def stage_logs(kind, p):
        bias = bias_ref[kind]
        for hh in range(nh):
            z = zbuf[1 - p, hh] + bias
            soft = jnp.log(1.0 + jnp.exp2(jnp.abs(z) * (-LOG2_E)))
            neg_log_1m = jnp.maximum(z, 0.0) + soft
            lbuf[p, hh] = z - neg_log_1m
            pbuf[p, hh] = neg_log_1m.astype(BF16)
            row_sum = -jnp.sum(neg_log_1m, axis=1, keepdims=True)
            rbuf[p, hh] = jnp.broadcast_to(row_sum, (t, LANES))

    def stage_weights(first, p):
        for hh in range(nh):
            tail = _dot(pbuf[1 - p, hh], u_ref[:, 0:t])
            row_sum = rbuf[1 - p, hh]
            if first:
                carry_ref[hh] = row_sum
            else:
                carry = carry_ref[hh]
                tail = tail + jnp.concatenate([carry] * (t // LANES), axis=1)
                carry_ref[hh] = carry + row_sum
            wbuf[p, hh] = jnp.exp(lbuf[1 - p, hh] + tail).astype(BF16)

    def stage_output(q_off, k_off, grp, first, p):
        k_off = pl.multiple_of(k_off, t)
        w_cat = jnp.concatenate([wbuf[1 - p, hh] for hh in range(nh)], axis=1)
        v_cat = jnp.concatenate([v_all[grp * nh + hh, pl.ds(k_off, t), :]
                                 for hh in range(nh)], axis=0)
        acc = _dot(w_cat, v_cat)
        if not first:
            acc = acc_ref[...] + acc
            o_ref[pl.ds(pl.multiple_of(q_off, t), t),
                  pl.ds(pl.multiple_of(grp * LANES, LANES), LANES)] = acc.astype(BF16)
        acc_ref[...] = acc

    def field(item, f):
        return tab_ref[item * _ITEM_FIELDS + f]

    def iteration(i, parity):
        p = parity
        stage_logs(field(i + 2, 2), p)
        if parity == 0:
            stage_weights(True, p)
        else:
            stage_weights(False, p)
            carry_all[field(i + 1, 3)] = carry_ref[...]
        stage_scores(field(i + 3, 0), field(i + 3, 1), field(i + 3, 4), p)
        stage_output(field(i, 0), field(i, 1), field(i, 4), parity == 1, p)
        if parity == 0:
            acc_all[field(i, 3)] = acc_ref[...]

    def body(k, c):
        for j in range(_UNROLL):
            iteration(_UNROLL * k + j, j % 2)
        return c

    lax.fori_loop(0, n_iter // _UNROLL, body, 0)

    def saturated():
        return jnp.max(carry_ref[...]) <= LOG_UNDERFLOW

    n_tiles = s // t
    first_open = _PIPELINED_BLOCKS - 1

    def finish_tile(k, c):
        slot = open_ref[k]
        grp = slot // n_tiles
        qi = slot % n_tiles
        n_blocks = qi + 2
        carry_ref[...] = carry_all[slot]
        acc_ref[...] = acc_all[slot]
        q_off = qi * t

        def more(state):
            n, done = state
            return jnp.logical_and(n < n_blocks, jnp.logical_not(done))

        def one_block(state):
            n, _ = state
            is_meta = n == n_blocks - 1
            k_off = jnp.where(is_meta, s, (qi - n) * t)
            kind = jnp.where(is_meta, _KIND_META, _KIND_PLAIN)
            stage_scores(q_off, k_off, grp, 0)
            stage_logs(kind, 1)
            stage_weights(None, 0)
            stage_output(q_off, k_off, grp, None, 1)
            return n + 1, saturated()

        lax.while_loop(more, one_block, (jnp.int32(_PIPELINED_BLOCKS), jnp.bool_(False)))
        return c

    open_slots = [grp * n_tiles + qi for grp in range(STEP_GROUPS)
                  for qi in range(first_open, n_tiles)]
    worst = carry_all[open_slots[0]]
    for slot in open_slots[1:]:
        worst = jnp.maximum(worst, carry_all[slot])

    @pl.when(jnp.max(worst) > LOG_UNDERFLOW)
    def _():
        n_open = jnp.int32(0)
        for slot in open_slots:
            open_ref[n_open] = slot
            stays = jnp.max(carry_all[slot]) > LOG_UNDERFLOW
            n_open = n_open + stays.astype(jnp.int32)
        lax.fori_loop(0, n_open, finish_tile, 0)


def _attention(qkv, kt, meta_qkv, meta_kt, *, t):
    b, s, _ = qkv.shape
    d = D_MODEL
    nh = HEADS_PER_GROUP
    table, n_iter = _attention_items(s, t)
    u = _suffix_sum_matrix(t, -1.0)
    bias = _mask_bias(t)
    width = STEP_GROUPS * LANES
    col0 = d // width
    n_slots = STEP_GROUPS * (s // t)
    grid_spec = pltpu.PrefetchScalarGridSpec(
        num_scalar_prefetch=1,
        grid=(b, N_GROUPS // STEP_GROUPS),
        in_specs=[
            pl.BlockSpec((None, s, width), lambda bi, g, tab: (bi, 0, g)),
            pl.BlockSpec((None, width, s), lambda bi, g, tab: (bi, g, 0)),
            pl.BlockSpec((None, s, width), lambda bi, g, tab: (bi, 0, 2 * col0 + g)),
            pl.BlockSpec((width, META_ROWS), lambda bi, g, tab: (g, 0)),
            pl.BlockSpec((META_ROWS, width), lambda bi, g, tab: (0, 2 * col0 + g)),
            pl.BlockSpec(u.shape, lambda bi, g, tab: (0, 0)),
            pl.BlockSpec(bias.shape, lambda bi, g, tab: (0, 0, 0)),
        ],
        out_specs=pl.BlockSpec((None, s, width), lambda bi, g, tab: (bi, 0, g)),
        scratch_shapes=[
            pltpu.VMEM((STEP_GROUPS * nh, LANES, s + t), BF16),
            pltpu.VMEM((STEP_GROUPS * nh, s + t, LANES), BF16),
            pltpu.VMEM((2, nh, t, t), F32),
            pltpu.VMEM((2, nh, t, t), F32),
            pltpu.VMEM((2, nh, t, t), BF16),
            pltpu.VMEM((2, nh, t, t), BF16),
            pltpu.VMEM((nh, t, LANES), F32),
            pltpu.VMEM((t, LANES), F32),
            pltpu.VMEM((n_slots, nh, t, LANES), F32),
            pltpu.VMEM((n_slots, t, LANES), F32),
            pltpu.SMEM((n_slots,), jnp.int32),
            pltpu.VMEM((2, nh, t, LANES), F32),
        ],
    )
    kernel = functools.partial(_attn_pipe_kernel, s=s, t=t, n_iter=n_iter)
    return pl.pallas_call(
        kernel,
        grid_spec=grid_spec,
        out_shape=jax.ShapeDtypeStruct((b, s, d), BF16),
        compiler_params=_params("arbitrary", "arbitrary"),
        name="stick_breaking_attention",
    )(jnp.asarray(table), qkv, kt, qkv, meta_kt, meta_qkv, u, bias)


def kernel(x, meta_tokens, conv_norm, conv_w_in, conv_w, conv_w_out, attn_norm,
           attn_w_qkv, attn_w_out, mlp_norm, mlp_w1, mlp_w2, final_norm):
    b, s, d = x.shape
    depth = mlp_norm.shape[0]
    hr = x.reshape(b * s, d)
    hm = jnp.pad(meta_tokens.astype(x.dtype), ((0, META_ROWS - N_META), (0, 0)))
    zero_hist = jnp.zeros((SUBLANES, d), F32)
    gain = lambda g: g.reshape(1, d)

    stacks = {"w_in": conv_w_in, "w_out": conv_w_out, "w_qkv": attn_w_qkv,
              "w_o": attn_w_out, "w1": mlp_w1, "w2": mlp_w2}
    is_conv = lambda i: i % 2 == 0

    def mixer_weights(i):
        return [("w_in", i // 2), ("w_out", i // 2)] if is_conv(i) else [("w_qkv", i // 2)]

    def mlp_weights(i):
        return ([] if is_conv(i) else [("w_o", i // 2)]) + [("w1", i), ("w2", i)]

    first = mixer_weights(0) + ([] if is_conv(0) else mlp_weights(0))
    bf = {key: stacks[key[0]][key[1]].astype(BF16) for key in first}

    def with_casts(call, keys):
        out, *cast = call(casts=[(stacks[name], layer) for name, layer in keys])
        bf.update(zip(keys, cast))
        return out

    for i in range(depth):
        j = i // 2
        last = i == depth - 1
        final_g = gain(final_norm) if last else None
        g_mlp = gain(mlp_norm[i])
        nxt = [] if last else mixer_weights(i + 1) + ([] if is_conv(i + 1) else mlp_weights(i + 1))
        if is_conv(i):
            g = gain(conv_norm[j])
            w_in, w_out = bf["w_in", j], bf["w_out", j]
            hm, hist = _conv_mixer(hm, g, w_in, conv_w[j], w_out, zero_hist,
                                   tm=META_ROWS, tiles_per_seq=1, hist_row=N_META)
            hr = with_casts(functools.partial(
                _conv_mixer, hr, g, w_in, conv_w[j], w_out, hist,
                tm=WIDE_ROW_TILE, tiles_per_seq=s // WIDE_ROW_TILE), mlp_weights(i))
            (hm,) = _mlp(hm, g_mlp, bf["w1", i], bf["w2", i], tm=META_ROWS)
            hr = with_casts(functools.partial(
                _mlp, hr, g_mlp, bf["w1", i], bf["w2", i], tm=ROW_TILE, final_g=final_g), nxt)
        else:
            g = gain(attn_norm[j])
            qkv_m, kt_m = _qkv(hm, g, bf["w_qkv", j], tm=META_ROWS, seq=META_ROWS)
            qkv_r, kt_r = _qkv(hr, g, bf["w_qkv", j], tm=WIDE_ROW_TILE, seq=s)
            o_r = _attention(qkv_r.reshape(b, s, 3 * d), kt_r, qkv_m, kt_m[0], t=ATTN_TILE)
            if not last:
                o_m = _attention_small(qkv_m)
                (hm,) = _mlp(hm, g_mlp, bf["w1", i], bf["w2", i], tm=META_ROWS,
                             attn=o_m, w_o=bf["w_o", j])
            hr = with_casts(functools.partial(
                _mlp, hr, g_mlp, bf["w1", i], bf["w2", i], tm=ROW_TILE,
                attn=o_r.reshape(b * s, d), w_o=bf["w_o", j], final_g=final_g), nxt)
    return hr.reshape(b, s, d)
```
